```python
import jax, jax.numpy as jnp
from jax import lax
import numpy as np

D_MODEL = 1024
BATCH = 4
SEQ = 4096
DEPTH = 2

CHUNK = 64
CONV_WIDTH = 3
D_CONV = 1024
D_GMLP = 1024
GMLP_BLOCK = 128
N_GROUPS_GMLP = 8
D_POOL = 1024
POOL_WINDOWS = (2, 4, 8, 16)
POOL_GROUP = D_POOL // len(POOL_WINDOWS)
N_BRANCHES = 3
D_FF = 2816
D_IN = 3 * D_CONV + 2 * D_GMLP + D_POOL + N_BRANCHES * D_MODEL
ALPHA = (2 * DEPTH) ** 0.25
BETA = (8 * DEPTH) ** -0.25
LN_EPS = 1e-5

kernel_name = "hybrid_conv_gmlp_pool_deepnorm_adaln"


def layer_norm(x, g, b):
    xf = x.astype(jnp.float32)
    mu = jnp.mean(xf, axis=-1, keepdims=True)
    var = jnp.mean(jnp.square(xf - mu), axis=-1, keepdims=True)
    y = (xf - mu) * lax.rsqrt(var + LN_EPS)
    return (y * g.astype(jnp.float32) + b.astype(jnp.float32)).astype(x.dtype)


def causal_dwconv(x, w):
    k, ch = w.shape
    return lax.conv_general_dilated(
        x, w[:, None, :].astype(x.dtype), window_strides=(1,), padding=[(k - 1, 0)],
        dimension_numbers=("NWC", "WIO", "NWC"), feature_group_count=ch)


def spatial_gating(u, v, ln_g, ln_b, w_s, b_s):
    bn, s, _ = v.shape
    v = layer_norm(v, ln_g, ln_b)
    vb = v.reshape(bn, s // GMLP_BLOCK, GMLP_BLOCK, N_GROUPS_GMLP, D_GMLP // N_GROUPS_GMLP)
    pos = jnp.arange(GMLP_BLOCK)
    allowed = (pos[None, :] // CHUNK) <= (pos[:, None] // CHUNK)
    w = jnp.where(allowed[None], w_s, jnp.zeros_like(w_s))
    mixed = jnp.einsum("gij,bnjgc->bnigc", w, vb) + b_s.T[None, None, :, :, None]
    return u * mixed.reshape(bn, s, D_GMLP)


def multiscale_pool(p, w_pool, scale):
    s = p.shape[1]
    pf = p.astype(jnp.float32)
    cs = jnp.cumsum(pf, axis=1)
    t = jnp.arange(1, s + 1, dtype=jnp.float32)
    outs = []
    for k, win in enumerate(POOL_WINDOWS):
        lo, hi = k * POOL_GROUP, (k + 1) * POOL_GROUP
        csk = cs[..., lo:hi]
        prev = jnp.pad(csk, ((0, 0), (win, 0), (0, 0)))[:, :s]
        mean = (csk - prev) / jnp.minimum(t, float(win))[None, :, None]
        d = (mean - pf[..., lo:hi]).astype(p.dtype)
        outs.append(d @ w_pool[k])
    return jnp.concatenate(outs, axis=-1) * scale


def setup_inputs(seed: int = 0) -> dict:
    key = jax.random.key(seed)
    ks = jax.random.split(key, 26)
    f32 = jnp.float32

    def nrm(k, shape, s):
        return jax.random.normal(k, shape, f32) * s

    L = DEPTH
    w_in_scale = D_MODEL ** -0.5
    return {
        "x": nrm(ks[0], (BATCH, SEQ, D_MODEL), 1.0),
        "c": nrm(ks[1], (BATCH, D_MODEL), 1.0),
        "w_ada": nrm(ks[2], (L, D_MODEL, 6 * D_MODEL), 0.5 * D_MODEL ** -0.5),
        "b_ada": nrm(ks[3], (L, 6 * D_MODEL), 0.02),
        "w_in": nrm(ks[4], (L, D_MODEL, D_IN), w_in_scale),
        "b_in": nrm(ks[5], (L, D_IN), 0.01),
        "conv_a": nrm(ks[6], (L, CONV_WIDTH, D_CONV), 0.5),
        "w_a_out": nrm(ks[7], (L, D_CONV, D_MODEL), D_CONV ** -0.5),
        "ln_v_g": 1.0 + nrm(ks[8], (L, D_GMLP), 0.02),
        "ln_v_b": nrm(ks[9], (L, D_GMLP), 0.02),
        "w_spatial": nrm(ks[10], (L, N_GROUPS_GMLP, GMLP_BLOCK, GMLP_BLOCK), 0.5 * GMLP_BLOCK ** -0.5),
        "b_spatial": 1.0 + nrm(ks[11], (L, N_GROUPS_GMLP, GMLP_BLOCK), 0.02),
        "w_b_out": nrm(ks[12], (L, D_GMLP, D_MODEL), D_GMLP ** -0.5),
        "w_pool": nrm(ks[13], (L, len(POOL_WINDOWS), POOL_GROUP, POOL_GROUP), POOL_GROUP ** -0.5),
        "pool_scale": 1.0 + nrm(ks[14], (L, D_POOL), 0.02),
        "w_o": nrm(ks[15], (L, D_MODEL, D_MODEL), BETA * D_MODEL ** -0.5),
        "ln1_g": 1.0 + nrm(ks[16], (L, D_MODEL), 0.02),
        "ln1_b": nrm(ks[17], (L, D_MODEL), 0.02),
        "w_up": nrm(ks[18], (L, D_MODEL, 2 * D_FF), w_in_scale),
        "b_up": nrm(ks[19], (L, 2 * D_FF), 0.01),
        "conv_ffn": nrm(ks[20], (L, CONV_WIDTH, D_FF), 0.5),
        "conv_ffn_b": nrm(ks[21], (L, D_FF), 0.01),
        "w_down": nrm(ks[22], (L, D_FF, D_MODEL), BETA * D_FF ** -0.5),
        "ln2_g": 1.0 + nrm(ks[23], (L, D_MODEL), 0.02),
        "ln2_b": nrm(ks[24], (L, D_MODEL), 0.02),
    }


def reference(x, c, w_ada, b_ada, w_in, b_in, conv_a, w_a_out, ln_v_g, ln_v_b,
              w_spatial, b_spatial, w_b_out, w_pool, pool_scale, w_o, ln1_g, ln1_b,
              w_up, b_up, conv_ffn, conv_ffn_b, w_down, ln2_g, ln2_b):
    split_points = [D_CONV, 2 * D_CONV, 3 * D_CONV, 3 * D_CONV + D_GMLP,
                    3 * D_CONV + 2 * D_GMLP, 3 * D_CONV + 2 * D_GMLP + D_POOL]
    c_act = jax.nn.silu(c)
    for l in range(DEPTH):
        ada = (c_act @ w_ada[l] + b_ada[l])[:, None, :]
        sh1, sc1, gt1, sh2, sc2, gt2 = jnp.split(ada, 6, axis=-1)

        h = x * (1.0 + sc1) + sh1
        z = h @ w_in[l] + b_in[l]
        zb, zc, zx, zu, zv, zp, zg = jnp.split(z, split_points, axis=-1)
        y_a = (zb * causal_dwconv(zc * zx, conv_a[l])) @ w_a_out[l]
        y_b = spatial_gating(jax.nn.gelu(zu), jax.nn.gelu(zv), ln_v_g[l], ln_v_b[l],
                             w_spatial[l], b_spatial[l]) @ w_b_out[l]
        y_c = multiscale_pool(zp, w_pool[l], pool_scale[l])
        g_a, g_b, g_c = jnp.split(jax.nn.sigmoid(zg), 3, axis=-1)
        merged = g_a * y_a + g_b * y_b + g_c * y_c
        x = layer_norm(ALPHA * x + gt1 * (merged @ w_o[l]), ln1_g[l], ln1_b[l])

        h = x * (1.0 + sc2) + sh2
        up_a, up_g = jnp.split(h @ w_up[l] + b_up[l], 2, axis=-1)
        f = jax.nn.gelu(causal_dwconv(up_a, conv_ffn[l]) + conv_ffn_b[l]) * up_g
        x = layer_norm(ALPHA * x + gt2 * (f @ w_down[l]), ln2_g[l], ln2_b[l])
    return x
```

```python
import functools

import jax
import jax.numpy as jnp
from jax.experimental import pallas as pl
from jax.experimental.pallas import tpu as pltpu

D_MODEL = 1024
DEPTH = 2
CHUNK = 64
CONV_WIDTH = 3
D_CONV = 1024
D_GMLP = 1024
GMLP_BLOCK = 128
N_GROUPS_GMLP = 8
GMLP_GROUP = D_GMLP // N_GROUPS_GMLP
D_POOL = 1024
POOL_WINDOWS = (2, 4, 8, 16)
POOL_GROUP = D_POOL // len(POOL_WINDOWS)
D_FF = 2816
D_IN = 3 * D_CONV + 2 * D_GMLP + D_POOL + 3 * D_MODEL
ALPHA = (2 * DEPTH) ** 0.25
LN_EPS = 1e-5

OFF_B, OFF_C, OFF_X = 0, D_CONV, 2 * D_CONV
OFF_U = 3 * D_CONV
OFF_V = OFF_U + D_GMLP
OFF_P = OFF_V + D_GMLP
OFF_G = OFF_P + D_POOL

SUBLANES = 8
SEQ_TILE = 256
CONV_HALO = SUBLANES
POOL_HALO = 32
COL_BLOCK = 512
V7X_VMEM_BYTES = 64 * 1024 * 1024
VMEM_LIMIT_BYTES = V7X_VMEM_BYTES * 3 // 4

_F32 = jnp.float32
_BF16 = jnp.bfloat16


def _gelu_tanh(x):
    cdf = 0.5 * (1.0 + jnp.tanh(0.7978845608028654 * (x + 0.044715 * (x * x * x))))
    return x * cdf


def _sigmoid(x):
    return 0.5 * (jnp.tanh(0.5 * x) + 1.0)


def _layer_norm(r, g, b):
    mu = jnp.mean(r, axis=-1, keepdims=True)
    d = r - mu
    var = jnp.mean(d * d, axis=-1, keepdims=True)
    return d * jax.lax.rsqrt(var + LN_EPS) * g + b


def _dot(a, b):
    return jnp.dot(a, b, preferred_element_type=_F32)


def _causal_conv3(scr, w, cols, ts):
    lo, n = cols
    base = CONV_HALO - (CONV_WIDTH - 1)
    acc = None
    for k in range(CONV_WIDTH):
        term = w[k:k + 1, :] * scr[base + k:base + k + ts, lo:lo + n]
        acc = term if acc is None else acc + term
    return acc


def _token_mix_kernel(x_ref, ada_ref, w_in_ref, b_in_ref, conv_a_ref, w_a_out_ref,
                      lnv_g_ref, lnv_b_ref, w_sp_ref, b_sp_ref, w_b_out_ref,
                      w_pool_ref, pscale_ref, w_o_ref, ln_g_ref, ln_b_ref,
                      o_ref,
                      conv_scr, pool_scr, tmp_a, tmp_b, a_scr, u_scr, v_scr, ub_scr,
                      merged_scr):
    ts = x_ref.shape[1]
    seq_tile = pl.program_id(1)

    @pl.when(seq_tile == 0)
    def _():
        conv_scr[0:CONV_HALO, :] = jnp.zeros((CONV_HALO, D_CONV), _F32)
        pool_scr[0:POOL_HALO, :] = jnp.zeros((POOL_HALO, D_POOL), _F32)

    x = x_ref[0]
    shift, scale, gate = ada_ref[0, 0:1, :], ada_ref[0, 1:2, :], ada_ref[0, 2:3, :]
    h = (x * (1.0 + scale) + shift).astype(_BF16)

    def proj(lo, n):
        return _dot(h, w_in_ref[:, lo:lo + n]) + b_in_ref[:, lo:lo + n]

    for lo in range(0, D_CONV, COL_BLOCK):
        conv_scr[CONV_HALO:CONV_HALO + ts, lo:lo + COL_BLOCK] = (
            proj(OFF_C + lo, COL_BLOCK) * proj(OFF_X + lo, COL_BLOCK))
        cv = _causal_conv3(conv_scr, conv_a_ref[:, lo:lo + COL_BLOCK], (lo, COL_BLOCK), ts)
        a_scr[:, lo:lo + COL_BLOCK] = (proj(OFF_B + lo, COL_BLOCK) * cv).astype(_BF16)
    conv_scr[0:CONV_HALO, :] = conv_scr[ts:ts + CONV_HALO, :]
    y_a = _dot(a_scr[...], w_a_out_ref[...])
    merged_scr[...] = _sigmoid(proj(OFF_G, D_MODEL)) * y_a

    u_scr[...] = _gelu_tanh(proj(OFF_U, D_GMLP))
    v = _layer_norm(_gelu_tanh(proj(OFF_V, D_GMLP)), lnv_g_ref[...], lnv_b_ref[...])
    v_scr[...] = v.astype(_BF16)
    chunk_shift = CHUNK.bit_length() - 1
    assert CHUNK == 1 << chunk_shift
    block_idx = (GMLP_BLOCK, GMLP_BLOCK)
    pos = jax.lax.broadcasted_iota(jnp.int32, block_idx, 0) >> chunk_shift
    src = jax.lax.broadcasted_iota(jnp.int32, block_idx, 1) >> chunk_shift
    allowed = src <= pos
    for g in range(N_GROUPS_GMLP):
        cols = slice(g * GMLP_GROUP, (g + 1) * GMLP_GROUP)
        w_g = jnp.where(allowed, w_sp_ref[g], jnp.zeros((), _BF16))
        for n in range(ts // GMLP_BLOCK):
            rows = slice(n * GMLP_BLOCK, (n + 1) * GMLP_BLOCK)
            mixed = _dot(w_g, v_scr[rows, cols]) + b_sp_ref[:, cols]
            ub_scr[rows, cols] = (u_scr[rows, cols] * mixed).astype(_BF16)
    y_b = _dot(ub_scr[...], w_b_out_ref[...])
    merged_scr[...] += _sigmoid(proj(OFF_G + D_MODEL, D_MODEL)) * y_b

    pool_scr[POOL_HALO:POOL_HALO + ts, :] = proj(OFF_P, D_POOL)
    row = seq_tile * ts + jax.lax.broadcasted_iota(jnp.int32, (ts, 1), 0)
    ext = POOL_HALO + ts
    for k, win in enumerate(POOL_WINDOWS):
        lo = k * POOL_GROUP
        cols = slice(lo, lo + POOL_GROUP)
        src_ref, src_cols, width = pool_scr, cols, 1
        bufs = (tmp_a, tmp_b)
        stage = 0
        while 2 * width < win:
            start = SUBLANES * (stage + 1)
            dst = bufs[stage % 2]
            dst[start:ext, :] = (src_ref[start:ext, src_cols]
                                 + src_ref[start - width:ext - width, src_cols])
            src_ref, src_cols = dst, slice(0, POOL_GROUP)
            width *= 2
            stage += 1
        wsum = (src_ref[POOL_HALO:ext, src_cols]
                + src_ref[POOL_HALO - width:ext - width, src_cols])
        denom = jnp.minimum(row + 1, win).astype(_F32)
        d = wsum / denom - pool_scr[POOL_HALO:ext, cols]
        y_c = _dot(d.astype(_BF16), w_pool_ref[k]) * pscale_ref[:, cols]
        gate_c = _sigmoid(proj(OFF_G + 2 * D_MODEL + lo, POOL_GROUP))
        merged_scr[:, cols] += gate_c * y_c
    pool_scr[0:POOL_HALO, :] = pool_scr[ts:ts + POOL_HALO, :]

    out = _dot(merged_scr[...].astype(_BF16), w_o_ref[...])
    o_ref[0] = _layer_norm(ALPHA * x + gate * out, ln_g_ref[...], ln_b_ref[...])


def _channel_mix_kernel(x_ref, ada_ref, w_up_ref, b_up_ref, conv_w_ref, conv_b_ref,
                        w_down_ref, ln_g_ref, ln_b_ref, o_ref, conv_scr, f_scr):
    ts = x_ref.shape[1]

    @pl.when(pl.program_id(1) == 0)
    def _():
        conv_scr[0:CONV_HALO, :] = jnp.zeros((CONV_HALO, D_FF), _F32)

    x = x_ref[0]
    shift, scale, gate = ada_ref[0, 3:4, :], ada_ref[0, 4:5, :], ada_ref[0, 5:6, :]
    h = (x * (1.0 + scale) + shift).astype(_BF16)

    def proj(lo, n):
        return _dot(h, w_up_ref[:, lo:lo + n]) + b_up_ref[:, lo:lo + n]

    for lo in range(0, D_FF, COL_BLOCK):
        n = min(COL_BLOCK, D_FF - lo)
        conv_scr[CONV_HALO:CONV_HALO + ts, lo:lo + n] = proj(lo, n)
        cv = _causal_conv3(conv_scr, conv_w_ref[:, lo:lo + n], (lo, n), ts)
        act = _gelu_tanh(cv + conv_b_ref[:, lo:lo + n])
        f_scr[:, lo:lo + n] = (act * proj(D_FF + lo, n)).astype(_BF16)
    conv_scr[0:CONV_HALO, :] = conv_scr[ts:ts + CONV_HALO, :]

    out = _dot(f_scr[...], w_down_ref[...])
    o_ref[0] = _layer_norm(ALPHA * x + gate * out, ln_g_ref[...], ln_b_ref[...])


def _ada_kernel(c_ref, w_ref, b_ref, o_ref):
    c = c_ref[...]
    c_act = (c * _sigmoid(c)).astype(_BF16)
    o_ref[0] = _dot(c_act, w_ref[0].astype(_BF16)) + b_ref[0]


def _resident(shape):
    zeros = (0,) * len(shape)
    return pl.BlockSpec(shape, lambda b, s: zeros, pipeline_mode=pl.Buffered(1))


def _seq_tiled_call(body, x, ada, weights, scratch_shapes, name):
    batch, seq, d = x.shape
    assert seq % SEQ_TILE == 0 and SEQ_TILE % GMLP_BLOCK == 0
    x_spec = pl.BlockSpec((1, SEQ_TILE, d), lambda b, s: (b, s, 0))
    ada_spec = pl.BlockSpec((1,) + ada.shape[1:], lambda b, s: (b, 0, 0))
    return pl.pallas_call(
        body,
        grid=(batch, seq // SEQ_TILE),
        in_specs=[x_spec, ada_spec] + [_resident(w.shape) for w in weights],
        out_specs=x_spec,
        out_shape=jax.ShapeDtypeStruct(x.shape, x.dtype),
        scratch_shapes=scratch_shapes,
        compiler_params=pltpu.CompilerParams(
            dimension_semantics=("arbitrary", "arbitrary"),
            vmem_limit_bytes=VMEM_LIMIT_BYTES),
        name=name,
    )(x, ada, *weights)


def _token_mix(x, ada, weights):
    ts = SEQ_TILE
    scratch = [
        pltpu.VMEM((CONV_HALO + ts, D_CONV), _F32),
        pltpu.VMEM((POOL_HALO + ts, D_POOL), _F32),
        pltpu.VMEM((POOL_HALO + ts, POOL_GROUP), _F32),
        pltpu.VMEM((POOL_HALO + ts, POOL_GROUP), _F32),
        pltpu.VMEM((ts, D_CONV), _BF16),
        pltpu.VMEM((ts, D_GMLP), _F32),
        pltpu.VMEM((ts, D_GMLP), _BF16),
        pltpu.VMEM((ts, D_GMLP), _BF16),
        pltpu.VMEM((ts, D_MODEL), _F32),
    ]
    return _seq_tiled_call(_token_mix_kernel, x, ada, weights, scratch, "token_mix")


def _channel_mix(x, ada, weights):
    ts = SEQ_TILE
    scratch = [
        pltpu.VMEM((CONV_HALO + ts, D_FF), _F32),
        pltpu.VMEM((ts, D_FF), _BF16),
    ]
    return _seq_tiled_call(_channel_mix_kernel, x, ada, weights, scratch, "channel_mix")


def _ada_modulation(c, w_ada, b_ada):
    depth, d, n_out = w_ada.shape
    batch = c.shape[0]
    rows = -(-batch // SUBLANES) * SUBLANES
    c_pad = jnp.pad(c, ((0, rows - batch), (0, 0)))
    col_block = d
    out = pl.pallas_call(
        _ada_kernel,
        grid=(depth, n_out // col_block),
        in_specs=[
            pl.BlockSpec((rows, d), lambda l, j: (0, 0)),
            pl.BlockSpec((1, d, col_block), lambda l, j: (l, 0, j)),
            pl.BlockSpec((1, 1, col_block), lambda l, j: (l, 0, j)),
        ],
        out_specs=pl.BlockSpec((1, rows, col_block), lambda l, j: (l, 0, j)),
        out_shape=jax.ShapeDtypeStruct((depth, rows, n_out), _F32),
        compiler_params=pltpu.CompilerParams(
            dimension_semantics=("arbitrary", "arbitrary")),
        name="ada_modulation",
    )(c_pad, w_ada, b_ada.reshape(depth, 1, n_out))
    return out[:, :batch, :].reshape(depth, batch, n_out // d, d)


def kernel(x, c, w_ada, b_ada, w_in, b_in, conv_a, w_a_out, ln_v_g, ln_v_b, w_spatial, b_spatial, w_b_out, w_pool, pool_scale, w_o, ln1_g, ln1_b, w_up, b_up, conv_ffn, conv_ffn_b, w_down, ln2_g, ln2_b):
    depth = w_in.shape[0]
    ada = _ada_modulation(c, w_ada, b_ada)

    def row(p):
        return p[:, None, :]

    b_sp = jnp.repeat(jnp.swapaxes(b_spatial, 1, 2), GMLP_GROUP, axis=2)

    w_in_h, w_a_out_h, w_sp_h, w_b_out_h, w_pool_h, w_o_h, w_up_h, w_down_h = (
        w.astype(_BF16) for w in (w_in, w_a_out, w_spatial, w_b_out, w_pool, w_o, w_up, w_down))

    for l in range(depth):
        x = _token_mix(x, ada[l], (
            w_in_h[l], row(b_in)[l], conv_a[l], w_a_out_h[l], row(ln_v_g)[l], row(ln_v_b)[l],
            w_sp_h[l], b_sp[l], w_b_out_h[l], w_pool_h[l], row(pool_scale)[l], w_o_h[l],
            row(ln1_g)[l], row(ln1_b)[l]))
        x = _channel_mix(x, ada[l], (
            w_up_h[l], row(b_up)[l], conv_ffn[l], row(conv_ffn_b)[l], w_down_h[l],
            row(ln2_g)[l], row(ln2_b)[l]))
    return x
```

```python
import jax
import jax.numpy as jnp
from jax.experimental import pallas as pl
from jax.experimental.pallas import tpu as pltpu

D_MODEL = 1024
DEPTH = 2
CHUNK = 64
CONV_WIDTH = 3
D_CONV = 1024
D_GMLP = 1024
GMLP_BLOCK = 128
N_GROUPS_GMLP = 8
GMLP_GROUP = D_GMLP // N_GROUPS_GMLP
D_POOL = 1024
POOL_WINDOWS = (2, 4, 8, 16)
POOL_GROUP = D_POOL // len(POOL_WINDOWS)
D_FF = 2816
D_IN = 3 * D_CONV + 2 * D_GMLP + D_POOL + 3 * D_MODEL
ALPHA = (2 * DEPTH) ** 0.25
LN_EPS = 1e-5

OFF_B, OFF_C, OFF_X = 0, D_CONV, 2 * D_CONV
OFF_U = 3 * D_CONV
OFF_V = OFF_U + D_GMLP
OFF_P = OFF_V + D_GMLP
OFF_G = OFF_P + D_POOL

SUBLANES = 8
V7X_MXU_DIM = 256
TOKEN_SEQ_TILE = 256
CHANNEL_SEQ_TILE = 512
ROW_BLOCK = 128
CONV_HALO = SUBLANES
POOL_HALO = 32
COL_BLOCK = 512
D_FF_SPLIT = 4 * COL_BLOCK
assert D_FF_SPLIT % V7X_MXU_DIM == 0 and (D_FF - D_FF_SPLIT) % V7X_MXU_DIM == 0
V7X_VMEM_BYTES = 64 * 1024 * 1024
VMEM_LIMIT_BYTES = V7X_VMEM_BYTES * 7 // 8

_F32 = jnp.float32
_BF16 = jnp.bfloat16


def _gelu_tanh(x):
    cdf = 0.5 * (1.0 + jnp.tanh(0.7978845608028654 * (x + 0.044715 * (x * x * x))))
    return x * cdf


def _sigmoid(x):
    return 0.5 * (jnp.tanh(0.5 * x) + 1.0)


def _layer_norm(r, g, b):
    mu = jnp.mean(r, axis=-1, keepdims=True)
    d = r - mu
    var = jnp.mean(d * d, axis=-1, keepdims=True)
    return d * jax.lax.rsqrt(var + LN_EPS) * g + b


def _dot(a, b):
    return jnp.dot(a, b, preferred_element_type=_F32)


def _col_blocks(lo, hi):
    return [(c, min(COL_BLOCK, hi - c)) for c in range(lo, hi, COL_BLOCK)]


def _causal_conv3(scr, w, lo, n, ts):
    base = CONV_HALO - (CONV_WIDTH - 1)
    acc = None
    for k in range(CONV_WIDTH):
        term = w[k:k + 1, :] * scr[base + k:base + k + ts, lo:lo + n]
        acc = term if acc is None else acc + term
    return acc


def _residual_post_norm(x_ref, o_ref, gate, ln_g_ref, ln_b_ref, sublayer_rows):
    ts = x_ref.shape[1]
    for r in range(0, ts, ROW_BLOCK):
        rows = slice(r, r + ROW_BLOCK)
        res = ALPHA * x_ref[0, rows, :] + gate * sublayer_rows(rows)
        o_ref[0, rows, :] = _layer_norm(res, ln_g_ref[...], ln_b_ref[...])


def _token_mix_kernel(x_ref, ada_ref, w_in_ref, b_in_ref, conv_a_ref, w_a_out_ref,
                      lnv_g_ref, lnv_b_ref, w_sp_ref, b_sp_ref, w_b_out_ref,
                      w_pool_ref, pscale_ref, w_o_ref, ln_g_ref, ln_b_ref,
                      o_ref,
                      s0, s1, s2, s3, s4, conv_scr, pool_scr, tmp_a, tmp_b, a_scr, v_scr,
                      ub_scr, d_scr, merged_scr):
    ts = x_ref.shape[1]
    seq_tile = pl.program_id(1)

    @pl.when(seq_tile == 0)
    def _():
        conv_scr[0:CONV_HALO, :] = jnp.zeros((CONV_HALO, D_CONV), _F32)
        pool_scr[0:POOL_HALO, :] = jnp.zeros((POOL_HALO, D_POOL), _F32)

    shift, scale, gate = ada_ref[0, 0:1, :], ada_ref[0, 1:2, :], ada_ref[0, 2:3, :]
    h = (x_ref[0] * (1.0 + scale) + shift).astype(_BF16)

    def project(slot, off):
        for c, n in _col_blocks(0, D_MODEL):
            slot[:, c:c + n] = _dot(h, w_in_ref[:, off + c:off + c + n])

    def z(slot, off, lo=0, n=D_MODEL):
        return slot[:, lo:lo + n] + b_in_ref[:, off + lo:off + lo + n]

    project(s0, OFF_V)
    project(s1, OFF_C)
    v = _layer_norm(_gelu_tanh(z(s0, OFF_V)), lnv_g_ref[...], lnv_b_ref[...])
    v_scr[...] = v.astype(_BF16)

    project(s2, OFF_X)
    project(s3, OFF_B)
    conv_scr[CONV_HALO:CONV_HALO + ts, :] = z(s1, OFF_C) * z(s2, OFF_X)

    project(s4, OFF_U)
    for lo, n in _col_blocks(0, D_CONV):
        cv = _causal_conv3(conv_scr, conv_a_ref[:, lo:lo + n], lo, n, ts)
        a_scr[:, lo:lo + n] = (z(s3, OFF_B, lo, n) * cv).astype(_BF16)
    conv_scr[0:CONV_HALO, :] = conv_scr[ts:ts + CONV_HALO, :]

    chunk_shift = CHUNK.bit_length() - 1
    assert CHUNK == 1 << chunk_shift
    block_idx = (GMLP_BLOCK, GMLP_BLOCK)
    pos = jax.lax.broadcasted_iota(jnp.int32, block_idx, 0) >> chunk_shift
    src = jax.lax.broadcasted_iota(jnp.int32, block_idx, 1) >> chunk_shift
    allowed = src <= pos
    for g in range(N_GROUPS_GMLP):
        cols = slice(g * GMLP_GROUP, (g + 1) * GMLP_GROUP)
        w_g = jnp.where(allowed, w_sp_ref[g], jnp.zeros((), _BF16))
        for n in range(ts // GMLP_BLOCK):
            rows = slice(n * GMLP_BLOCK, (n + 1) * GMLP_BLOCK)
            s0[rows, cols] = _dot(w_g, v_scr[rows, cols])

    project(s1, OFF_G)
    for n in range(ts // GMLP_BLOCK):
        rows = slice(n * GMLP_BLOCK, (n + 1) * GMLP_BLOCK)
        u = _gelu_tanh(s4[rows, :] + b_in_ref[:, OFF_U:OFF_U + D_GMLP])
        ub_scr[rows, :] = (u * (s0[rows, :] + b_sp_ref[...])).astype(_BF16)

    s2[...] = _dot(a_scr[...], w_a_out_ref[...])
    project(s3, OFF_P)
    merged_scr[...] = _sigmoid(z(s1, OFF_G)) * s2[...]

    project(s4, OFF_G + D_MODEL)
    pool_scr[POOL_HALO:POOL_HALO + ts, :] = z(s3, OFF_P)
    row = seq_tile * ts + jax.lax.broadcasted_iota(jnp.int32, (ts, 1), 0)
    ext = POOL_HALO + ts
    for k, win in enumerate(POOL_WINDOWS):
        cols = slice(k * POOL_GROUP, (k + 1) * POOL_GROUP)
        src_ref, src_cols, width = pool_scr, cols, 1
        bufs = (tmp_a, tmp_b)
        stage = 0
        while 2 * width < win:
            start = SUBLANES * (stage + 1)
            dst = bufs[stage % 2]
            dst[start:ext, :] = (src_ref[start:ext, src_cols]
                                 + src_ref[start - width:ext - width, src_cols])
            src_ref, src_cols = dst, slice(0, POOL_GROUP)
            width *= 2
            stage += 1
        wsum = (src_ref[POOL_HALO:ext, src_cols]
                + src_ref[POOL_HALO - width:ext - width, src_cols])
        denom = jnp.minimum(row + 1, win).astype(_F32)
        d_scr[:, cols] = (wsum / denom - pool_scr[POOL_HALO:ext, cols]).astype(_BF16)
    pool_scr[0:POOL_HALO, :] = pool_scr[ts:ts + POOL_HALO, :]

    s0[...] = _dot(ub_scr[...], w_b_out_ref[...])
    project(s1, OFF_G + 2 * D_MODEL)
    merged_scr[...] += _sigmoid(z(s4, OFF_G + D_MODEL)) * s0[...]

    for k in range(len(POOL_WINDOWS)):
        cols = slice(k * POOL_GROUP, (k + 1) * POOL_GROUP)
        s2[:, cols] = _dot(d_scr[:, cols], w_pool_ref[k])
    merged_scr[...] += _sigmoid(z(s1, OFF_G + 2 * D_MODEL)) * (s2[...] * pscale_ref[...])

    def merge_rows(rows):
        return _dot(merged_scr[rows, :].astype(_BF16), w_o_ref[...])

    _residual_post_norm(x_ref, o_ref, gate, ln_g_ref, ln_b_ref, merge_rows)


def _channel_mix_kernel(x_ref, ada_ref, w_up_ref, b_up_ref, conv_w_ref, conv_b_ref,
                        w_down_ref, ln_g_ref, ln_b_ref, o_ref,
                        up_scr, conv_scr, f_scr, acc_scr):
    ts = x_ref.shape[1]

    @pl.when(pl.program_id(1) == 0)
    def _():
        conv_scr[0:CONV_HALO, :] = jnp.broadcast_to(-b_up_ref[:, 0:D_FF], (CONV_HALO, D_FF))

    shift, scale, gate = ada_ref[0, 3:4, :], ada_ref[0, 4:5, :], ada_ref[0, 5:6, :]
    h = (x_ref[0] * (1.0 + scale) + shift).astype(_BF16)

    def project(lo, n):
        conv_scr[CONV_HALO:CONV_HALO + ts, lo:lo + n] = _dot(h, w_up_ref[:, lo:lo + n])
        up_scr[:, lo:lo + n] = _dot(h, w_up_ref[:, D_FF + lo:D_FF + lo + n])

    def gated_block(lo, n):
        w = conv_w_ref[:, lo:lo + n]
        bias = conv_b_ref[:, lo:lo + n] + b_up_ref[:, lo:lo + n] * jnp.sum(w, axis=0, keepdims=True)
        act = _gelu_tanh(_causal_conv3(conv_scr, w, lo, n, ts) + bias)
        up_g = up_scr[:, lo:lo + n] + b_up_ref[:, D_FF + lo:D_FF + lo + n]
        f_scr[:, lo:lo + n] = (act * up_g).astype(_BF16)

    blocks = _col_blocks(0, D_FF)
    assert D_FF_SPLIT // COL_BLOCK <= len(blocks) - 2
    project(*blocks[0])
    for i in range(1, len(blocks)):
        project(*blocks[i])
        if i == len(blocks) - 1:
            acc_scr[...] = _dot(f_scr[:, 0:D_FF_SPLIT], w_down_ref[0:D_FF_SPLIT, :])
        gated_block(*blocks[i - 1])
    gated_block(*blocks[-1])
    conv_scr[0:CONV_HALO, :] = conv_scr[ts:ts + CONV_HALO, :]

    def down_rows(rows):
        return acc_scr[rows, :] + _dot(f_scr[rows, D_FF_SPLIT:D_FF],
                                       w_down_ref[D_FF_SPLIT:D_FF, :])

    _residual_post_norm(x_ref, o_ref, gate, ln_g_ref, ln_b_ref, down_rows)


def _ada_kernel(c_ref, w_ref, b_ref, o_ref):
    c = c_ref[...]
    c_act = (c * _sigmoid(c)).astype(_BF16)
    o_ref[0] = _dot(c_act, w_ref[0].astype(_BF16)) + b_ref[0]


def _layer_resident(stacked, layer):
    tail = stacked.shape[1:]
    zeros = (0,) * len(tail)
    return pl.BlockSpec((None,) + tail, lambda b, s: (layer,) + zeros,
                        pipeline_mode=pl.Buffered(1))


def _seq_tiled_call(body, x, ada, layer, weights, scratch_shapes, name, seq_tile):
    batch, seq, d = x.shape
    assert seq % seq_tile == 0 and seq_tile % GMLP_BLOCK == 0 and seq_tile % ROW_BLOCK == 0
    x_spec = pl.BlockSpec((1, seq_tile, d), lambda b, s: (b, s, 0))
    ada_spec = pl.BlockSpec((None, 1) + ada.shape[2:], lambda b, s: (layer, b, 0, 0))
    return pl.pallas_call(
        body,
        grid=(batch, seq // seq_tile),
        in_specs=[x_spec, ada_spec] + [_layer_resident(w, layer) for w in weights],
        out_specs=x_spec,
        out_shape=jax.ShapeDtypeStruct(x.shape, x.dtype),
        scratch_shapes=scratch_shapes,
        compiler_params=pltpu.CompilerParams(
            dimension_semantics=("arbitrary", "arbitrary"),
            vmem_limit_bytes=VMEM_LIMIT_BYTES),
        name=name,
    )(x, ada, *weights)


def _token_mix(x, ada, layer, weights):
    ts = TOKEN_SEQ_TILE
    scratch = [pltpu.VMEM((ts, D_MODEL), _F32) for _ in range(5)] + [
        pltpu.VMEM((CONV_HALO + ts, D_CONV), _F32),
        pltpu.VMEM((POOL_HALO + ts, D_POOL), _F32),
        pltpu.VMEM((POOL_HALO + ts, POOL_GROUP), _F32),
        pltpu.VMEM((POOL_HALO + ts, POOL_GROUP), _F32),
        pltpu.VMEM((ts, D_CONV), _BF16),
        pltpu.VMEM((ts, D_GMLP), _BF16),
        pltpu.VMEM((ts, D_GMLP), _BF16),
        pltpu.VMEM((ts, D_POOL), _BF16),
        pltpu.VMEM((ts, D_MODEL), _F32),
    ]
    return _seq_tiled_call(_token_mix_kernel, x, ada, layer, weights, scratch, "token_mix", ts)


def _channel_mix(x, ada, layer, weights):
    ts = CHANNEL_SEQ_TILE
    scratch = [
        pltpu.VMEM((ts, D_FF), _F32),
        pltpu.VMEM((CONV_HALO + ts, D_FF), _F32),
        pltpu.VMEM((ts, D_FF), _BF16),
        pltpu.VMEM((ts, D_MODEL), _F32),
    ]
    return _seq_tiled_call(_channel_mix_kernel, x, ada, layer, weights, scratch, "channel_mix", ts)


def _ada_modulation(c, w_ada, b_ada):
    depth, d, n_out = w_ada.shape
    batch = c.shape[0]
    rows = -(-batch // SUBLANES) * SUBLANES
    c_pad = jnp.pad(c, ((0, rows - batch), (0, 0)))
    col_block = d
    out = pl.pallas_call(
        _ada_kernel,
        grid=(depth, n_out // col_block),
        in_specs=[
            pl.BlockSpec((rows, d), lambda l, j: (0, 0)),
            pl.BlockSpec((1, d, col_block), lambda l, j: (l, 0, j)),
            pl.BlockSpec((1, 1, col_block), lambda l, j: (l, 0, j)),
        ],
        out_specs=pl.BlockSpec((1, rows, col_block), lambda l, j: (l, 0, j)),
        out_shape=jax.ShapeDtypeStruct((depth, rows, n_out), _F32),
        compiler_params=pltpu.CompilerParams(
            dimension_semantics=("arbitrary", "arbitrary")),
        name="ada_modulation",
    )(c_pad, w_ada, b_ada.reshape(depth, 1, n_out))
    return out[:, :batch, :].reshape(depth, batch, n_out // d, d)


def kernel(x, c, w_ada, b_ada, w_in, b_in, conv_a, w_a_out, ln_v_g, ln_v_b, w_spatial, b_spatial, w_b_out, w_pool, pool_scale, w_o, ln1_g, ln1_b, w_up, b_up, conv_ffn, conv_ffn_b, w_down, ln2_g, ln2_b):
    depth = w_in.shape[0]
    ada = _ada_modulation(c, w_ada, b_ada)

    def row(p):
        return p[:, None, :]

    b_sp = jnp.repeat(jnp.swapaxes(b_spatial, 1, 2), GMLP_GROUP, axis=2)

    w_in_h, w_a_out_h, w_sp_h, w_b_out_h, w_pool_h, w_o_h, w_up_h, w_down_h = (
        w.astype(_BF16) for w in (w_in, w_a_out, w_spatial, w_b_out, w_pool, w_o, w_up, w_down))

    token_weights = (w_in_h, row(b_in), conv_a, w_a_out_h, row(ln_v_g), row(ln_v_b), w_sp_h,
                     b_sp, w_b_out_h, w_pool_h, row(pool_scale), w_o_h, row(ln1_g), row(ln1_b))
    channel_weights = (w_up_h, row(b_up), conv_ffn, row(conv_ffn_b), w_down_h,
                       row(ln2_g), row(ln2_b))
    for layer in range(depth):
        x = _token_mix(x, ada, layer, token_weights)
        x = _channel_mix(x, ada, layer, channel_weights)
    return x
```

```python
import jax
import jax.numpy as jnp
from jax.experimental import pallas as pl
from jax.experimental.pallas import tpu as pltpu

D_MODEL = 1024
DEPTH = 2
CHUNK = 64
CONV_WIDTH = 3
D_CONV = 1024
D_GMLP = 1024
GMLP_BLOCK = 128
N_GROUPS_GMLP = 8
GMLP_GROUP = D_GMLP // N_GROUPS_GMLP
D_POOL = 1024
POOL_WINDOWS = (2, 4, 8, 16)
POOL_GROUP = D_POOL // len(POOL_WINDOWS)
D_FF = 2816
D_IN = 3 * D_CONV + 2 * D_GMLP + D_POOL + 3 * D_MODEL
ALPHA = (2 * DEPTH) ** 0.25
LN_EPS = 1e-5

OFF_B, OFF_C, OFF_X = 0, D_CONV, 2 * D_CONV
OFF_U = 3 * D_CONV
OFF_V = OFF_U + D_GMLP
OFF_P = OFF_V + D_GMLP
OFF_G = OFF_P + D_POOL

SUBLANES = 8
V7X_MXU_DIM = 256
SEQ_TILE = 512
ROW_BLOCK = 128
CONV_HALO = SUBLANES
POOL_HALO = 32
COL_BLOCK = 512
D_FF_SPLIT = 4 * COL_BLOCK
assert D_FF_SPLIT % V7X_MXU_DIM == 0 and (D_FF - D_FF_SPLIT) % V7X_MXU_DIM == 0
CAST_ROWS = 1024
V7X_VMEM_BYTES = 64 * 1024 * 1024
VMEM_LIMIT_BYTES = V7X_VMEM_BYTES * 15 // 16

_F32 = jnp.float32
_BF16 = jnp.bfloat16


def _gelu_tanh(x):
    cdf = 0.5 * (1.0 + jnp.tanh(0.7978845608028654 * (x + 0.044715 * (x * x * x))))
    return x * cdf


def _sigmoid(x):
    return 0.5 * (jnp.tanh(0.5 * x) + 1.0)


def _layer_norm(r, g, b):
    mu = jnp.mean(r, axis=-1, keepdims=True)
    d = r - mu
    var = jnp.mean(d * d, axis=-1, keepdims=True)
    return d * jax.lax.rsqrt(var + LN_EPS) * g + b


def _dot(a, b):
    return jnp.dot(a, b, preferred_element_type=_F32)


def _col_blocks(lo, hi):
    return [(c, min(COL_BLOCK, hi - c)) for c in range(lo, hi, COL_BLOCK)]


def _causal_conv3(scr, w, lo, n, ts):
    base = CONV_HALO - (CONV_WIDTH - 1)
    acc = None
    for k in range(CONV_WIDTH):
        term = w[k:k + 1, :] * scr[base + k:base + k + ts, lo:lo + n]
        acc = term if acc is None else acc + term
    return acc


def _residual_post_norm(x_ref, o_ref, gate, ln_g_ref, ln_b_ref, sublayer_rows):
    ts = x_ref.shape[1]
    for r in range(0, ts, ROW_BLOCK):
        rows = slice(r, r + ROW_BLOCK)
        res = ALPHA * x_ref[0, rows, :] + gate * sublayer_rows(rows)
        o_ref[0, rows, :] = _layer_norm(res, ln_g_ref[...], ln_b_ref[...])


def _token_mix_kernel(x_ref, ada_ref, w_in_ref, b_in_ref, conv_a_ref, w_a_out_ref,
                      lnv_g_ref, lnv_b_ref, w_sp_ref, b_sp_ref, w_b_out_ref,
                      w_pool_ref, pscale_ref, w_o_ref, ln_g_ref, ln_b_ref,
                      o_ref,
                      s0, s1, s2, s3, conv_scr, pool_scr, tmp_a, tmp_b, a_scr, v_scr,
                      merged_scr):
    ts = x_ref.shape[1]
    seq_tile = pl.program_id(1)

    @pl.when(seq_tile == 0)
    def _():
        conv_scr[0:CONV_HALO, :] = jnp.zeros((CONV_HALO, D_CONV), _F32)
        pool_scr[0:POOL_HALO, :] = jnp.zeros((POOL_HALO, D_POOL), _F32)

    shift, scale, gate = ada_ref[0, 0:1, :], ada_ref[0, 1:2, :], ada_ref[0, 2:3, :]
    h = (x_ref[0] * (1.0 + scale) + shift).astype(_BF16)

    def project(slot, off):
        for c, n in _col_blocks(0, D_MODEL):
            slot[:, c:c + n] = _dot(h, w_in_ref[:, off + c:off + c + n])

    def z(slot, off, lo=0, n=D_MODEL):
        return slot[:, lo:lo + n] + b_in_ref[:, off + lo:off + lo + n]

    project(s0, OFF_V)
    project(s1, OFF_C)
    v = _layer_norm(_gelu_tanh(z(s0, OFF_V)), lnv_g_ref[...], lnv_b_ref[...])
    v_scr[...] = v.astype(_BF16)

    project(s2, OFF_X)
    project(s3, OFF_B)
    conv_scr[CONV_HALO:CONV_HALO + ts, :] = z(s1, OFF_C) * z(s2, OFF_X)

    project(s1, OFF_U)
    for lo, n in _col_blocks(0, D_CONV):
        cv = _causal_conv3(conv_scr, conv_a_ref[:, lo:lo + n], lo, n, ts)
        a_scr[:, lo:lo + n] = (z(s3, OFF_B, lo, n) * cv).astype(_BF16)
    conv_scr[0:CONV_HALO, :] = conv_scr[ts:ts + CONV_HALO, :]

    chunk_shift = CHUNK.bit_length() - 1
    assert CHUNK == 1 << chunk_shift
    block_idx = (GMLP_BLOCK, GMLP_BLOCK)
    pos = jax.lax.broadcasted_iota(jnp.int32, block_idx, 0) >> chunk_shift
    src = jax.lax.broadcasted_iota(jnp.int32, block_idx, 1) >> chunk_shift
    allowed = src <= pos
    for g in range(N_GROUPS_GMLP):
        cols = slice(g * GMLP_GROUP, (g + 1) * GMLP_GROUP)
        w_g = jnp.where(allowed, w_sp_ref[g], jnp.zeros((), _BF16))
        for n in range(ts // GMLP_BLOCK):
            rows = slice(n * GMLP_BLOCK, (n + 1) * GMLP_BLOCK)
            s0[rows, cols] = _dot(w_g, v_scr[rows, cols])

    project(s2, OFF_G)
    s3[...] = _dot(a_scr[...], w_a_out_ref[...])
    for n in range(ts // GMLP_BLOCK):
        rows = slice(n * GMLP_BLOCK, (n + 1) * GMLP_BLOCK)
        u = _gelu_tanh(s1[rows, :] + b_in_ref[:, OFF_U:OFF_U + D_GMLP])
        v_scr[rows, :] = (u * (s0[rows, :] + b_sp_ref[...])).astype(_BF16)

    project(s0, OFF_P)
    merged_scr[...] = _sigmoid(z(s2, OFF_G)) * s3[...]

    project(s1, OFF_G + D_MODEL)
    pool_scr[POOL_HALO:POOL_HALO + ts, :] = z(s0, OFF_P)
    row = seq_tile * ts + jax.lax.broadcasted_iota(jnp.int32, (ts, 1), 0)
    ext = POOL_HALO + ts
    for k, win in enumerate(POOL_WINDOWS):
        cols = slice(k * POOL_GROUP, (k + 1) * POOL_GROUP)
        src_ref, src_cols, width = pool_scr, cols, 1
        bufs = (tmp_a, tmp_b)
        stage = 0
        while 2 * width < win:
            start = SUBLANES * (stage + 1)
            dst = bufs[stage % 2]
            dst[start:ext, :] = (src_ref[start:ext, src_cols]
                                 + src_ref[start - width:ext - width, src_cols])
            src_ref, src_cols = dst, slice(0, POOL_GROUP)
            width *= 2
            stage += 1
        wsum = (src_ref[POOL_HALO:ext, src_cols]
                + src_ref[POOL_HALO - width:ext - width, src_cols])
        denom = jnp.minimum(row + 1, win).astype(_F32)
        a_scr[:, cols] = (wsum / denom - pool_scr[POOL_HALO:ext, cols]).astype(_BF16)
    pool_scr[0:POOL_HALO, :] = pool_scr[ts:ts + POOL_HALO, :]

    s2[...] = _dot(v_scr[...], w_b_out_ref[...])
    project(s3, OFF_G + 2 * D_MODEL)
    merged_scr[...] += _sigmoid(z(s1, OFF_G + D_MODEL)) * s2[...]

    for k in range(len(POOL_WINDOWS)):
        cols = slice(k * POOL_GROUP, (k + 1) * POOL_GROUP)
        s0[:, cols] = _dot(a_scr[:, cols], w_pool_ref[k])
    merged_scr[...] += _sigmoid(z(s3, OFF_G + 2 * D_MODEL)) * (s0[...] * pscale_ref[...])

    def merge_rows(rows):
        return _dot(merged_scr[rows, :].astype(_BF16), w_o_ref[...])

    _residual_post_norm(x_ref, o_ref, gate, ln_g_ref, ln_b_ref, merge_rows)


def _channel_mix_kernel(x_ref, ada_ref, w_up_ref, b_up_ref, conv_w_ref, conv_b_ref,
                        w_down_ref, ln_g_ref, ln_b_ref, o_ref,
                        up_scr, conv_scr, f_scr, acc_scr):
    ts = x_ref.shape[1]

    @pl.when(pl.program_id(1) == 0)
    def _():
        conv_scr[0:CONV_HALO, :] = jnp.broadcast_to(-b_up_ref[:, 0:D_FF], (CONV_HALO, D_FF))

    shift, scale, gate = ada_ref[0, 3:4, :], ada_ref[0, 4:5, :], ada_ref[0, 5:6, :]
    h = (x_ref[0] * (1.0 + scale) + shift).astype(_BF16)

    def project(lo, n):
        conv_scr[CONV_HALO:CONV_HALO + ts, lo:lo + n] = _dot(h, w_up_ref[:, lo:lo + n])
        up_scr[:, lo:lo + n] = _dot(h, w_up_ref[:, D_FF + lo:D_FF + lo + n])

    def gated_block(lo, n):
        w = conv_w_ref[:, lo:lo + n]
        bias = conv_b_ref[:, lo:lo + n] + b_up_ref[:, lo:lo + n] * jnp.sum(w, axis=0, keepdims=True)
        act = _gelu_tanh(_causal_conv3(conv_scr, w, lo, n, ts) + bias)
        up_g = up_scr[:, lo:lo + n] + b_up_ref[:, D_FF + lo:D_FF + lo + n]
        f_scr[:, lo:lo + n] = (act * up_g).astype(_BF16)

    blocks = _col_blocks(0, D_FF)
    assert D_FF_SPLIT // COL_BLOCK <= len(blocks) - 2
    project(*blocks[0])
    for i in range(1, len(blocks)):
        project(*blocks[i])
        if i == len(blocks) - 1:
            acc_scr[...] = _dot(f_scr[:, 0:D_FF_SPLIT], w_down_ref[0:D_FF_SPLIT, :])
        gated_block(*blocks[i - 1])
    gated_block(*blocks[-1])
    conv_scr[0:CONV_HALO, :] = conv_scr[ts:ts + CONV_HALO, :]

    def down_rows(rows):
        return acc_scr[rows, :] + _dot(f_scr[rows, D_FF_SPLIT:D_FF],
                                       w_down_ref[D_FF_SPLIT:D_FF, :])

    _residual_post_norm(x_ref, o_ref, gate, ln_g_ref, ln_b_ref, down_rows)


def _ada_kernel(c_ref, w_ref, b_ref, o_ref):
    c = c_ref[...]
    c_act = (c * _sigmoid(c)).astype(_BF16)
    o_ref[0] = _dot(c_act, w_ref[0].astype(_BF16)) + b_ref[0]


def _resident_spec(array, layer):
    if layer is None:
        zeros = (0,) * array.ndim
        return pl.BlockSpec(array.shape, lambda b, s: zeros, pipeline_mode=pl.Buffered(1))
    zeros = (0,) * (array.ndim - 1)
    return pl.BlockSpec((None,) + array.shape[1:], lambda b, s: (layer,) + zeros,
                        pipeline_mode=pl.Buffered(1))


def _seq_tiled_call(body, x, ada, layer, weights, scratch_shapes, name, cast=()):
    batch, seq, d = x.shape
    assert seq % SEQ_TILE == 0 and SEQ_TILE % GMLP_BLOCK == 0 and SEQ_TILE % ROW_BLOCK == 0
    tiles = seq // SEQ_TILE
    steps = batch * tiles
    assert CAST_ROWS % steps == 0
    cast_rows = CAST_ROWS // steps
    cast_views = [w.reshape(w.shape[0] * CAST_ROWS, w.size // (w.shape[0] * CAST_ROWS))
                  for w, _ in cast]
    n_in, n_cast = 2 + len(weights), len(cast)

    def cast_in_spec(view, src_layer):
        return pl.BlockSpec((cast_rows, view.shape[1]),
                            lambda b, s: (src_layer * steps + b * tiles + s, 0))

    def kernel_body(*refs):
        ins, cast_in = refs[:n_in], refs[n_in:n_in + n_cast]
        o_ref, cast_out = refs[n_in + n_cast], refs[n_in + n_cast + 1:n_in + 2 * n_cast + 1]
        for src, dst in zip(cast_in, cast_out):
            dst[...] = src[...].astype(_BF16)
        body(*ins, o_ref, *refs[n_in + 2 * n_cast + 1:])

    x_spec = pl.BlockSpec((1, SEQ_TILE, d), lambda b, s: (b, s, 0))
    ada_spec = pl.BlockSpec((None, 1) + ada.shape[2:], lambda b, s: (layer, b, 0, 0))
    cast_in_specs = [cast_in_spec(v, l) for v, (_, l) in zip(cast_views, cast)]
    cast_out_specs = [pl.BlockSpec((cast_rows, v.shape[1]), lambda b, s: (b * tiles + s, 0))
                      for v in cast_views]
    outs = pl.pallas_call(
        kernel_body,
        grid=(batch, tiles),
        in_specs=([x_spec, ada_spec] + [_resident_spec(w, l) for w, l in weights]
                  + cast_in_specs),
        out_specs=[x_spec] + cast_out_specs,
        out_shape=([jax.ShapeDtypeStruct(x.shape, x.dtype)]
                   + [jax.ShapeDtypeStruct((CAST_ROWS, v.shape[1]), _BF16) for v in cast_views]),
        scratch_shapes=scratch_shapes,
        compiler_params=pltpu.CompilerParams(
            dimension_semantics=("arbitrary", "arbitrary"),
            vmem_limit_bytes=VMEM_LIMIT_BYTES),
        name=name,
    )(x, ada, *[w for w, _ in weights], *cast_views)
    return outs[0], [o.reshape(w.shape[1:]) for o, (w, _) in zip(outs[1:], cast)]


def _token_mix(x, ada, layer, weights, cast):
    ts = SEQ_TILE
    scratch = [pltpu.VMEM((ts, D_MODEL), _F32) for _ in range(4)] + [
        pltpu.VMEM((CONV_HALO + ts, D_CONV), _F32),
        pltpu.VMEM((POOL_HALO + ts, D_POOL), _F32),
        pltpu.VMEM((POOL_HALO + ts, POOL_GROUP), _F32),
        pltpu.VMEM((POOL_HALO + ts, POOL_GROUP), _F32),
        pltpu.VMEM((ts, D_CONV), _BF16),
        pltpu.VMEM((ts, D_GMLP), _BF16),
        pltpu.VMEM((ts, D_MODEL), _F32),
    ]
    return _seq_tiled_call(_token_mix_kernel, x, ada, layer, weights, scratch, "token_mix", cast)


def _channel_mix(x, ada, layer, weights, cast):
    ts = SEQ_TILE
    scratch = [
        pltpu.VMEM((ts, D_FF), _F32),
        pltpu.VMEM((CONV_HALO + ts, D_FF), _F32),
        pltpu.VMEM((ts, D_FF), _BF16),
        pltpu.VMEM((ts, D_MODEL), _F32),
    ]
    return _seq_tiled_call(_channel_mix_kernel, x, ada, layer, weights, scratch, "channel_mix",
                           cast)


def _ada_modulation(c, w_ada, b_ada):
    depth, d, n_out = w_ada.shape
    batch = c.shape[0]
    rows = -(-batch // SUBLANES) * SUBLANES
    c_pad = jnp.pad(c, ((0, rows - batch), (0, 0)))
    col_block = d
    out = pl.pallas_call(
        _ada_kernel,
        grid=(depth, n_out // col_block),
        in_specs=[
            pl.BlockSpec((rows, d), lambda l, j: (0, 0)),
            pl.BlockSpec((1, d, col_block), lambda l, j: (l, 0, j)),
            pl.BlockSpec((1, 1, col_block), lambda l, j: (l, 0, j)),
        ],
        out_specs=pl.BlockSpec((1, rows, col_block), lambda l, j: (l, 0, j)),
        out_shape=jax.ShapeDtypeStruct((depth, rows, n_out), _F32),
        compiler_params=pltpu.CompilerParams(
            dimension_semantics=("arbitrary", "arbitrary")),
        name="ada_modulation",
    )(c_pad, w_ada, b_ada.reshape(depth, 1, n_out))
    return out[:, :batch, :].reshape(depth, batch, n_out // d, d)


def kernel(x, c, w_ada, b_ada, w_in, b_in, conv_a, w_a_out, ln_v_g, ln_v_b, w_spatial, b_spatial, w_b_out, w_pool, pool_scale, w_o, ln1_g, ln1_b, w_up, b_up, conv_ffn, conv_ffn_b, w_down, ln2_g, ln2_b):
    depth = w_in.shape[0]
    ada = _ada_modulation(c, w_ada, b_ada)

    def row(p):
        return p[:, None, :]

    b_sp = jnp.repeat(jnp.swapaxes(b_spatial, 1, 2), GMLP_GROUP, axis=2)

    def token_weights(layer, w_in_h, w_a_out_h, w_sp_h, w_b_out_h, w_pool_h, w_o_h):
        return [(w_in_h, None), (row(b_in), layer), (conv_a, layer), (w_a_out_h, None),
                (row(ln_v_g), layer), (row(ln_v_b), layer), (w_sp_h, None), (b_sp, layer),
                (w_b_out_h, None), (w_pool_h, None), (row(pool_scale), layer), (w_o_h, None),
                (row(ln1_g), layer), (row(ln1_b), layer)]

    def channel_weights(layer, w_up_h, w_down_h):
        return [(w_up_h, None), (row(b_up), layer), (conv_ffn, layer), (row(conv_ffn_b), layer),
                (w_down_h, None), (row(ln2_g), layer), (row(ln2_b), layer)]

    token_f32 = (w_in, w_a_out, w_spatial, w_b_out, w_pool, w_o)
    channel_f32 = (w_up, w_down)

    token_h = [w[0].astype(_BF16) for w in token_f32]
    for layer in range(depth):
        x, channel_h = _token_mix(x, ada, layer, token_weights(layer, *token_h),
                                  [(w, layer) for w in channel_f32])
        nxt = [(w, layer + 1) for w in token_f32] if layer + 1 < depth else []
        x, token_h = _channel_mix(x, ada, layer, channel_weights(layer, *channel_h), nxt)
    return x
```

```python
import jax
import jax.numpy as jnp
from jax.experimental import pallas as pl
from jax.experimental.pallas import tpu as pltpu

D_MODEL = 1024
DEPTH = 2
CHUNK = 64
CONV_WIDTH = 3
D_CONV = 1024
D_GMLP = 1024
GMLP_BLOCK = 128
N_GROUPS_GMLP = 8
GMLP_GROUP = D_GMLP // N_GROUPS_GMLP
D_POOL = 1024
POOL_WINDOWS = (2, 4, 8, 16)
POOL_GROUP = D_POOL // len(POOL_WINDOWS)
D_FF = 2816
D_IN = 3 * D_CONV + 2 * D_GMLP + D_POOL + 3 * D_MODEL
ALPHA = (2 * DEPTH) ** 0.25
LN_EPS = 1e-5

OFF_B, OFF_C, OFF_X = 0, D_CONV, 2 * D_CONV
OFF_U = 3 * D_CONV
OFF_V = OFF_U + D_GMLP
OFF_P = OFF_V + D_GMLP
OFF_G = OFF_P + D_POOL

SUBLANES = 8
V7X_MXU_DIM = 256
SEQ_TILE = 512
ROW_BLOCK = 256
CONV_HALO = SUBLANES
POOL_HALO = 32
COL_BLOCK = 512
D_FF_SPLIT = 4 * COL_BLOCK
assert D_FF_SPLIT % V7X_MXU_DIM == 0 and (D_FF - D_FF_SPLIT) % V7X_MXU_DIM == 0
BF16_SUBLANES = 2 * SUBLANES
V7X_VMEM_BYTES = 64 * 1024 * 1024
VMEM_LIMIT_BYTES = V7X_VMEM_BYTES * 15 // 16

_F32 = jnp.float32
_BF16 = jnp.bfloat16


def _gelu_tanh(x):
    cdf = 0.5 * (1.0 + jnp.tanh(0.7978845608028654 * (x + 0.044715 * (x * x * x))))
    return x * cdf


def _sigmoid(x):
    return 0.5 * (jnp.tanh(0.5 * x) + 1.0)


def _layer_norm(r, g, b):
    mu = jnp.mean(r, axis=-1, keepdims=True)
    d = r - mu
    var = jnp.mean(d * d, axis=-1, keepdims=True)
    return d * jax.lax.rsqrt(var + LN_EPS) * g + b


def _dot(a, b):
    return jnp.dot(a, b, preferred_element_type=_F32)


def _col_blocks(lo, hi):
    return [(c, min(COL_BLOCK, hi - c)) for c in range(lo, hi, COL_BLOCK)]


def _causal_conv3(scr, w, lo, n, ts):
    base = CONV_HALO - (CONV_WIDTH - 1)
    acc = None
    for k in range(CONV_WIDTH):
        term = w[k:k + 1, :] * scr[base + k:base + k + ts, lo:lo + n]
        acc = term if acc is None else acc + term
    return acc


def _residual_post_norm(x_ref, o_ref, gate, ln_g_ref, ln_b_ref, sublayer_rows):
    ts = x_ref.shape[1]
    for r in range(0, ts, ROW_BLOCK):
        rows = slice(r, r + ROW_BLOCK)
        res = ALPHA * x_ref[0, rows, :] + gate * sublayer_rows(rows)
        o_ref[0, rows, :] = _layer_norm(res, ln_g_ref[...], ln_b_ref[...])


def _token_mix_kernel(x_ref, ada_ref, w_in_ref, b_in_ref, conv_a_ref, w_a_out_ref,
                      lnv_g_ref, lnv_b_ref, w_sp_ref, b_sp_ref, w_b_out_ref,
                      w_pool_ref, pscale_ref, w_o_ref, ln_g_ref, ln_b_ref,
                      o_ref,
                      s0, s1, s2, s3, conv_scr, pool_scr, tmp_a, tmp_b, a_scr, v_scr,
                      merged_scr):
    ts = x_ref.shape[1]
    seq_tile = pl.program_id(1)

    @pl.when(seq_tile == 0)
    def _():
        conv_scr[0:CONV_HALO, :] = jnp.zeros((CONV_HALO, D_CONV), _F32)
        pool_scr[0:POOL_HALO, :] = jnp.zeros((POOL_HALO, D_POOL), _F32)

    shift, scale, gate = ada_ref[0, 0:1, :], ada_ref[0, 1:2, :], ada_ref[0, 2:3, :]
    h = (x_ref[0] * (1.0 + scale) + shift).astype(_BF16)

    def project(slot, off):
        for c, n in _col_blocks(0, D_MODEL):
            slot[:, c:c + n] = _dot(h, w_in_ref[:, off + c:off + c + n])

    def z(slot, off, lo=0, n=D_MODEL):
        return slot[:, lo:lo + n] + b_in_ref[:, off + lo:off + lo + n]

    project(s0, OFF_V)
    project(s1, OFF_C)
    v = _layer_norm(_gelu_tanh(z(s0, OFF_V)), lnv_g_ref[...], lnv_b_ref[...])
    v_scr[...] = v.astype(_BF16)

    project(s2, OFF_X)
    project(s3, OFF_B)
    conv_scr[CONV_HALO:CONV_HALO + ts, :] = z(s1, OFF_C) * z(s2, OFF_X)

    project(s1, OFF_U)
    for lo, n in _col_blocks(0, D_CONV):
        cv = _causal_conv3(conv_scr, conv_a_ref[:, lo:lo + n], lo, n, ts)
        a_scr[:, lo:lo + n] = (z(s3, OFF_B, lo, n) * cv).astype(_BF16)
    conv_scr[0:CONV_HALO, :] = conv_scr[ts:ts + CONV_HALO, :]

    chunk_shift = CHUNK.bit_length() - 1
    assert CHUNK == 1 << chunk_shift
    block_idx = (GMLP_BLOCK, GMLP_BLOCK)
    pos = jax.lax.broadcasted_iota(jnp.int32, block_idx, 0) >> chunk_shift
    src = jax.lax.broadcasted_iota(jnp.int32, block_idx, 1) >> chunk_shift
    allowed = src <= pos
    for g in range(N_GROUPS_GMLP):
        cols = slice(g * GMLP_GROUP, (g + 1) * GMLP_GROUP)
        w_g = jnp.where(allowed, w_sp_ref[g], jnp.zeros((), _BF16))
        for n in range(ts // GMLP_BLOCK):
            rows = slice(n * GMLP_BLOCK, (n + 1) * GMLP_BLOCK)
            s0[rows, cols] = _dot(w_g, v_scr[rows, cols])

    project(s2, OFF_G)
    s3[...] = _dot(a_scr[...], w_a_out_ref[...])
    for n in range(ts // GMLP_BLOCK):
        rows = slice(n * GMLP_BLOCK, (n + 1) * GMLP_BLOCK)
        u = _gelu_tanh(s1[rows, :] + b_in_ref[:, OFF_U:OFF_U + D_GMLP])
        v_scr[rows, :] = (u * (s0[rows, :] + b_sp_ref[...])).astype(_BF16)

    project(s0, OFF_P)
    merged_scr[...] = _sigmoid(z(s2, OFF_G)) * s3[...]

    project(s1, OFF_G + D_MODEL)
    pool_scr[POOL_HALO:POOL_HALO + ts, :] = z(s0, OFF_P)
    row = seq_tile * ts + jax.lax.broadcasted_iota(jnp.int32, (ts, 1), 0)
    ext = POOL_HALO + ts
    for k, win in enumerate(POOL_WINDOWS):
        cols = slice(k * POOL_GROUP, (k + 1) * POOL_GROUP)
        src_ref, src_cols, width = pool_scr, cols, 1
        bufs = (tmp_a, tmp_b)
        stage = 0
        while 2 * width < win:
            start = SUBLANES * (stage + 1)
            dst = bufs[stage % 2]
            dst[start:ext, :] = (src_ref[start:ext, src_cols]
                                 + src_ref[start - width:ext - width, src_cols])
            src_ref, src_cols = dst, slice(0, POOL_GROUP)
            width *= 2
            stage += 1
        wsum = (src_ref[POOL_HALO:ext, src_cols]
                + src_ref[POOL_HALO - width:ext - width, src_cols])
        denom = jnp.minimum(row + 1, win).astype(_F32)
        a_scr[:, cols] = (wsum / denom - pool_scr[POOL_HALO:ext, cols]).astype(_BF16)
    pool_scr[0:POOL_HALO, :] = pool_scr[ts:ts + POOL_HALO, :]

    s2[...] = _dot(v_scr[...], w_b_out_ref[...])
    project(s3, OFF_G + 2 * D_MODEL)
    merged_scr[...] += _sigmoid(z(s1, OFF_G + D_MODEL)) * s2[...]

    for k in range(len(POOL_WINDOWS)):
        cols = slice(k * POOL_GROUP, (k + 1) * POOL_GROUP)
        s0[:, cols] = _dot(a_scr[:, cols], w_pool_ref[k])
    merged_scr[...] += _sigmoid(z(s3, OFF_G + 2 * D_MODEL)) * (s0[...] * pscale_ref[...])

    def merge_rows(rows):
        return _dot(merged_scr[rows, :].astype(_BF16), w_o_ref[...])

    _residual_post_norm(x_ref, o_ref, gate, ln_g_ref, ln_b_ref, merge_rows)


def _channel_mix_kernel(x_ref, ada_ref, w_up_ref, b_up_ref, conv_w_ref, conv_b_ref,
                        w_down_ref, ln_g_ref, ln_b_ref, o_ref,
                        up_scr, conv_scr, f_scr, acc_scr):
    ts = x_ref.shape[1]

    @pl.when(pl.program_id(1) == 0)
    def _():
        conv_scr[0:CONV_HALO, :] = jnp.broadcast_to(-b_up_ref[:, 0:D_FF], (CONV_HALO, D_FF))

    shift, scale, gate = ada_ref[0, 3:4, :], ada_ref[0, 4:5, :], ada_ref[0, 5:6, :]
    h = (x_ref[0] * (1.0 + scale) + shift).astype(_BF16)

    def project(lo, n):
        conv_scr[CONV_HALO:CONV_HALO + ts, lo:lo + n] = _dot(h, w_up_ref[:, lo:lo + n])
        up_scr[:, lo:lo + n] = _dot(h, w_up_ref[:, D_FF + lo:D_FF + lo + n])

    def gated_block(lo, n):
        w = conv_w_ref[:, lo:lo + n]
        bias = conv_b_ref[:, lo:lo + n] + b_up_ref[:, lo:lo + n] * jnp.sum(w, axis=0, keepdims=True)
        act = _gelu_tanh(_causal_conv3(conv_scr, w, lo, n, ts) + bias)
        up_g = up_scr[:, lo:lo + n] + b_up_ref[:, D_FF + lo:D_FF + lo + n]
        f_scr[:, lo:lo + n] = (act * up_g).astype(_BF16)

    blocks = _col_blocks(0, D_FF)
    assert D_FF_SPLIT // COL_BLOCK <= len(blocks) - 2
    project(*blocks[0])
    for i in range(1, len(blocks)):
        project(*blocks[i])
        if i == len(blocks) - 1:
            acc_scr[...] = _dot(f_scr[:, 0:D_FF_SPLIT], w_down_ref[0:D_FF_SPLIT, :])
        gated_block(*blocks[i - 1])
    gated_block(*blocks[-1])
    conv_scr[0:CONV_HALO, :] = conv_scr[ts:ts + CONV_HALO, :]

    def down_rows(rows):
        return acc_scr[rows, :] + _dot(f_scr[rows, D_FF_SPLIT:D_FF],
                                       w_down_ref[D_FF_SPLIT:D_FF, :])

    _residual_post_norm(x_ref, o_ref, gate, ln_g_ref, ln_b_ref, down_rows)


def _ada_kernel(c_ref, w_ref, b_ref, o_ref):
    c = c_ref[...]
    c_act = (c * _sigmoid(c)).astype(_BF16)
    o_ref[0] = _dot(c_act, w_ref[0].astype(_BF16)) + b_ref[0]


def _resident_spec(array, layer):
    if layer is None:
        zeros = (0,) * array.ndim
        return pl.BlockSpec(array.shape, lambda b, s: zeros, pipeline_mode=pl.Buffered(1))
    zeros = (0,) * (array.ndim - 1)
    return pl.BlockSpec((None,) + array.shape[1:], lambda b, s: (layer,) + zeros,
                        pipeline_mode=pl.Buffered(1))


def _seq_tiled_call(body, x, ada, layer, weights, scratch_shapes, name, cast=()):
    batch, seq, d = x.shape
    assert seq % SEQ_TILE == 0 and SEQ_TILE % GMLP_BLOCK == 0 and SEQ_TILE % ROW_BLOCK == 0
    tiles = seq // SEQ_TILE
    steps = batch * tiles
    n_in, n_cast = 2 + len(weights), len(cast)

    def cast_plan(w, src_layer):
        cols = w.shape[-1]
        rows = w.size // (w.shape[0] * cols)
        view = w.reshape(w.shape[0], rows, cols)
        stay = next(k for k in range(1, steps + 1)
                    if steps % k == 0 and (rows * k) % (steps * BF16_SUBLANES) == 0)
        slab = rows * stay // steps
        src = pl.BlockSpec((None, slab, cols),
                           lambda b, s: (src_layer, (b * tiles + s) // stay, 0))
        dst = pl.BlockSpec((slab, cols), lambda b, s: ((b * tiles + s) // stay, 0))
        return view, src, dst, jax.ShapeDtypeStruct((rows, cols), _BF16)

    plans = [cast_plan(w, l) for w, l in cast]

    def kernel_body(*refs):
        ins, cast_in = refs[:n_in], refs[n_in:n_in + n_cast]
        o_ref, cast_out = refs[n_in + n_cast], refs[n_in + n_cast + 1:n_in + 2 * n_cast + 1]
        for src, dst in zip(cast_in, cast_out):
            dst[...] = src[...].astype(_BF16)
        body(*ins, o_ref, *refs[n_in + 2 * n_cast + 1:])

    x_spec = pl.BlockSpec((1, SEQ_TILE, d), lambda b, s: (b, s, 0))
    ada_spec = pl.BlockSpec((None, 1) + ada.shape[2:], lambda b, s: (layer, b, 0, 0))
    outs = pl.pallas_call(
        kernel_body,
        grid=(batch, tiles),
        in_specs=([x_spec, ada_spec] + [_resident_spec(w, l) for w, l in weights]
                  + [p[1] for p in plans]),
        out_specs=[x_spec] + [p[2] for p in plans],
        out_shape=[jax.ShapeDtypeStruct(x.shape, x.dtype)] + [p[3] for p in plans],
        scratch_shapes=scratch_shapes,
        compiler_params=pltpu.CompilerParams(
            dimension_semantics=("arbitrary", "arbitrary"),
            vmem_limit_bytes=VMEM_LIMIT_BYTES),
        name=name,
    )(x, ada, *[w for w, _ in weights], *[p[0] for p in plans])
    return outs[0], [o.reshape(w.shape[1:]) for o, (w, _) in zip(outs[1:], cast)]


def _token_mix(x, ada, layer, weights, cast):
    ts = SEQ_TILE
    scratch = [pltpu.VMEM((ts, D_MODEL), _F32) for _ in range(4)] + [
        pltpu.VMEM((CONV_HALO + ts, D_CONV), _F32),
        pltpu.VMEM((POOL_HALO + ts, D_POOL), _F32),
        pltpu.VMEM((POOL_HALO + ts, POOL_GROUP), _F32),
        pltpu.VMEM((POOL_HALO + ts, POOL_GROUP), _F32),
        pltpu.VMEM((ts, D_CONV), _BF16),
        pltpu.VMEM((ts, D_GMLP), _BF16),
        pltpu.VMEM((ts, D_MODEL), _F32),
    ]
    return _seq_tiled_call(_token_mix_kernel, x, ada, layer, weights, scratch, "token_mix", cast)


def _channel_mix(x, ada, layer, weights, cast):
    ts = SEQ_TILE
    scratch = [
        pltpu.VMEM((ts, D_FF), _F32),
        pltpu.VMEM((CONV_HALO + ts, D_FF), _F32),
        pltpu.VMEM((ts, D_FF), _BF16),
        pltpu.VMEM((ts, D_MODEL), _F32),
    ]
    return _seq_tiled_call(_channel_mix_kernel, x, ada, layer, weights, scratch, "channel_mix",
                           cast)


def _ada_modulation(c, w_ada, b_ada):
    depth, d, n_out = w_ada.shape
    batch = c.shape[0]
    rows = -(-batch // SUBLANES) * SUBLANES
    c_pad = jnp.pad(c, ((0, rows - batch), (0, 0)))
    col_block = d
    out = pl.pallas_call(
        _ada_kernel,
        grid=(depth, n_out // col_block),
        in_specs=[
            pl.BlockSpec((rows, d), lambda l, j: (0, 0)),
            pl.BlockSpec((1, d, col_block), lambda l, j: (l, 0, j)),
            pl.BlockSpec((1, 1, col_block), lambda l, j: (l, 0, j)),
        ],
        out_specs=pl.BlockSpec((1, rows, col_block), lambda l, j: (l, 0, j)),
        out_shape=jax.ShapeDtypeStruct((depth, rows, n_out), _F32),
        compiler_params=pltpu.CompilerParams(
            dimension_semantics=("arbitrary", "arbitrary")),
        name="ada_modulation",
    )(c_pad, w_ada, b_ada.reshape(depth, 1, n_out))
    return out[:, :batch, :].reshape(depth, batch, n_out // d, d)


def kernel(x, c, w_ada, b_ada, w_in, b_in, conv_a, w_a_out, ln_v_g, ln_v_b, w_spatial, b_spatial, w_b_out, w_pool, pool_scale, w_o, ln1_g, ln1_b, w_up, b_up, conv_ffn, conv_ffn_b, w_down, ln2_g, ln2_b):
    depth = w_in.shape[0]
    ada = _ada_modulation(c, w_ada, b_ada)

    def row(p):
        return p[:, None, :]

    b_sp = jnp.repeat(jnp.swapaxes(b_spatial, 1, 2), GMLP_GROUP, axis=2)

    def token_weights(layer, w_in_h, w_a_out_h, w_sp_h, w_b_out_h, w_pool_h, w_o_h):
        return [(w_in_h, None), (row(b_in), layer), (conv_a, layer), (w_a_out_h, None),
                (row(ln_v_g), layer), (row(ln_v_b), layer), (w_sp_h, None), (b_sp, layer),
                (w_b_out_h, None), (w_pool_h, None), (row(pool_scale), layer), (w_o_h, None),
                (row(ln1_g), layer), (row(ln1_b), layer)]

    def channel_weights(layer, w_up_h, w_down_h):
        return [(w_up_h, None), (row(b_up), layer), (conv_ffn, layer), (row(conv_ffn_b), layer),
                (w_down_h, None), (row(ln2_g), layer), (row(ln2_b), layer)]

    token_f32 = (w_in, w_a_out, w_spatial, w_b_out, w_pool, w_o)
    channel_f32 = (w_up, w_down)

    token_h = [w[0].astype(_BF16) for w in token_f32]
    for layer in range(depth):
        x, channel_h = _token_mix(x, ada, layer, token_weights(layer, *token_h),
                                  [(w, layer) for w in channel_f32])
        nxt = [(w, layer + 1) for w in token_f32] if layer + 1 < depth else []
        x, token_h = _channel_mix(x, ada, layer, channel_weights(layer, *channel_h), nxt)
    return x
```

```python
import jax
import jax.numpy as jnp
from jax.experimental import pallas as pl
from jax.experimental.pallas import tpu as pltpu

D_MODEL = 1024
DEPTH = 2
CHUNK = 64
CONV_WIDTH = 3
D_CONV = 1024
D_GMLP = 1024
GMLP_BLOCK = 128
N_GROUPS_GMLP = 8
GMLP_GROUP = D_GMLP // N_GROUPS_GMLP
D_POOL = 1024
POOL_WINDOWS = (2, 4, 8, 16)
POOL_GROUP = D_POOL // len(POOL_WINDOWS)
D_FF = 2816
D_IN = 3 * D_CONV + 2 * D_GMLP + D_POOL + 3 * D_MODEL
ALPHA = (2 * DEPTH) ** 0.25
LN_EPS = 1e-5
GELU_C = 0.7978845608028654
GELU_A = 0.044715

OFF_B, OFF_C, OFF_X = 0, D_CONV, 2 * D_CONV
OFF_U = 3 * D_CONV
OFF_V = OFF_U + D_GMLP
OFF_P = OFF_V + D_GMLP
OFF_G = OFF_P + D_POOL

SUBLANES = 8
V7X_MXU_DIM = 256
SEQ_TILE = 512
ROW_BLOCK = 256
CONV_HALO = SUBLANES
POOL_HALO = 32
COL_BLOCK = 512
D_FF_SPLIT = 4 * COL_BLOCK
assert D_FF_SPLIT % V7X_MXU_DIM == 0 and (D_FF - D_FF_SPLIT) % V7X_MXU_DIM == 0
BF16_SUBLANES = 2 * SUBLANES
V7X_VMEM_BYTES = 64 * 1024 * 1024
VMEM_LIMIT_BYTES = V7X_VMEM_BYTES * 15 // 16

_F32 = jnp.float32
_BF16 = jnp.bfloat16


def _gelu_tanh(x):
    inner = x * (GELU_C + (GELU_C * GELU_A) * (x * x))
    half = 0.5 * x
    return half + half * jnp.tanh(inner)


def _sigmoid(x):
    return 0.5 * (jnp.tanh(0.5 * x) + 1.0)


def _layer_norm(r, g, b):
    mu = jnp.mean(r, axis=-1, keepdims=True)
    d = r - mu
    var = jnp.mean(d * d, axis=-1, keepdims=True)
    return d * jax.lax.rsqrt(var + LN_EPS) * g + b


def _dot(a, b):
    return jnp.dot(a, b, preferred_element_type=_F32)


def _col_blocks(lo, hi):
    return [(c, min(COL_BLOCK, hi - c)) for c in range(lo, hi, COL_BLOCK)]


def _causal_conv3(scr, w, lo, n, ts):
    base = CONV_HALO - (CONV_WIDTH - 1)
    acc = None
    for k in range(CONV_WIDTH):
        term = w[k:k + 1, :] * scr[base + k:base + k + ts, lo:lo + n]
        acc = term if acc is None else acc + term
    return acc


def _residual_post_norm(x_ref, o_ref, gate, ln_g_ref, ln_b_ref, sublayer_rows):
    ts = x_ref.shape[1]
    for r in range(0, ts, ROW_BLOCK):
        rows = slice(r, r + ROW_BLOCK)
        res = ALPHA * x_ref[0, rows, :] + gate * sublayer_rows(rows)
        o_ref[0, rows, :] = _layer_norm(res, ln_g_ref[...], ln_b_ref[...])


def _token_mix_kernel(x_ref, ada_ref, w_in_ref, b_in_ref, conv_a_ref, w_a_out_ref,
                      lnv_g_ref, lnv_b_ref, w_sp_ref, b_sp_ref, w_b_out_ref,
                      w_pool_ref, pscale_ref, w_o_ref, ln_g_ref, ln_b_ref,
                      o_ref,
                      s0, s1, s2, s3, conv_scr, pool_scr, tmp_a, tmp_b, a_scr, v_scr,
                      merged_scr):
    ts = x_ref.shape[1]
    seq_tile = pl.program_id(1)

    @pl.when(seq_tile == 0)
    def _():
        conv_scr[0:CONV_HALO, :] = jnp.zeros((CONV_HALO, D_CONV), _F32)
        pool_scr[0:POOL_HALO, :] = jnp.zeros((POOL_HALO, D_POOL), _F32)

    shift, scale, gate = ada_ref[0, 0:1, :], ada_ref[0, 1:2, :], ada_ref[0, 2:3, :]
    h = (x_ref[0] * (1.0 + scale) + shift).astype(_BF16)

    def project(slot, off):
        for c, n in _col_blocks(0, D_MODEL):
            slot[:, c:c + n] = _dot(h, w_in_ref[:, off + c:off + c + n])

    def z(slot, off, lo=0, n=D_MODEL):
        return slot[:, lo:lo + n] + b_in_ref[:, off + lo:off + lo + n]

    project(s0, OFF_C)
    project(s1, OFF_X)
    project(s2, OFF_B)
    conv_scr[CONV_HALO:CONV_HALO + ts, :] = z(s0, OFF_C) * z(s1, OFF_X)

    project(s3, OFF_V)
    for lo, n in _col_blocks(0, D_CONV):
        cv = _causal_conv3(conv_scr, conv_a_ref[:, lo:lo + n], lo, n, ts)
        a_scr[:, lo:lo + n] = (z(s2, OFF_B, lo, n) * cv).astype(_BF16)
    conv_scr[0:CONV_HALO, :] = conv_scr[ts:ts + CONV_HALO, :]

    project(s0, OFF_U)
    v = _layer_norm(_gelu_tanh(z(s3, OFF_V)), lnv_g_ref[...], lnv_b_ref[...])
    v_scr[...] = v.astype(_BF16)

    project(s1, OFF_G)
    s2[...] = _dot(a_scr[...], w_a_out_ref[...])

    chunk_shift = CHUNK.bit_length() - 1
    assert CHUNK == 1 << chunk_shift
    block_idx = (GMLP_BLOCK, GMLP_BLOCK)
    pos = jax.lax.broadcasted_iota(jnp.int32, block_idx, 0) >> chunk_shift
    src = jax.lax.broadcasted_iota(jnp.int32, block_idx, 1) >> chunk_shift
    allowed = src <= pos
    for g in range(N_GROUPS_GMLP):
        cols = slice(g * GMLP_GROUP, (g + 1) * GMLP_GROUP)
        w_g = jnp.where(allowed, w_sp_ref[g], jnp.zeros((), _BF16))
        n_blocks = ts // GMLP_BLOCK
        rhs = jnp.concatenate([v_scr[n * GMLP_BLOCK:(n + 1) * GMLP_BLOCK, cols]
                               for n in range(n_blocks)], axis=1)
        mixed = _dot(w_g, rhs)
        for n in range(n_blocks):
            s3[n * GMLP_BLOCK:(n + 1) * GMLP_BLOCK, cols] = (
                mixed[:, n * GMLP_BLOCK:(n + 1) * GMLP_BLOCK])

    merged_scr[...] = _sigmoid(z(s1, OFF_G)) * s2[...]
    project(s1, OFF_P)
    for n in range(ts // GMLP_BLOCK):
        rows = slice(n * GMLP_BLOCK, (n + 1) * GMLP_BLOCK)
        u = _gelu_tanh(s0[rows, :] + b_in_ref[:, OFF_U:OFF_U + D_GMLP])
        v_scr[rows, :] = (u * (s3[rows, :] + b_sp_ref[...])).astype(_BF16)

    project(s2, OFF_G + D_MODEL)
    pool_scr[POOL_HALO:POOL_HALO + ts, :] = z(s1, OFF_P)
    row = seq_tile * ts + jax.lax.broadcasted_iota(jnp.int32, (ts, 1), 0)
    ext = POOL_HALO + ts
    for k, win in enumerate(POOL_WINDOWS):
        cols = slice(k * POOL_GROUP, (k + 1) * POOL_GROUP)
        src_ref, src_cols, width = pool_scr, cols, 1
        bufs = (tmp_a, tmp_b)
        stage = 0
        while 2 * width < win:
            start = SUBLANES * (stage + 1)
            dst = bufs[stage % 2]
            dst[start:ext, :] = (src_ref[start:ext, src_cols]
                                 + src_ref[start - width:ext - width, src_cols])
            src_ref, src_cols = dst, slice(0, POOL_GROUP)
            width *= 2
            stage += 1
        wsum = (src_ref[POOL_HALO:ext, src_cols]
                + src_ref[POOL_HALO - width:ext - width, src_cols])
        denom = jnp.minimum(row + 1, win).astype(_F32)
        a_scr[:, cols] = (wsum / denom - pool_scr[POOL_HALO:ext, cols]).astype(_BF16)
    pool_scr[0:POOL_HALO, :] = pool_scr[ts:ts + POOL_HALO, :]

    s0[...] = _dot(v_scr[...], w_b_out_ref[...])
    project(s3, OFF_G + 2 * D_MODEL)
    merged_scr[...] += _sigmoid(z(s2, OFF_G + D_MODEL)) * s0[...]

    for k in range(len(POOL_WINDOWS)):
        cols = slice(k * POOL_GROUP, (k + 1) * POOL_GROUP)
        s1[:, cols] = _dot(a_scr[:, cols], w_pool_ref[k])
    merged_scr[...] += _sigmoid(z(s3, OFF_G + 2 * D_MODEL)) * (s1[...] * pscale_ref[...])

    def merge_rows(rows):
        return _dot(merged_scr[rows, :].astype(_BF16), w_o_ref[...])

    _residual_post_norm(x_ref, o_ref, gate, ln_g_ref, ln_b_ref, merge_rows)


def _channel_mix_kernel(x_ref, ada_ref, w_up_ref, b_up_ref, conv_w_ref, conv_b_ref,
                        w_down_ref, ln_g_ref, ln_b_ref, o_ref,
                        up_scr, conv_scr, f_scr, acc_scr):
    ts = x_ref.shape[1]

    @pl.when(pl.program_id(1) == 0)
    def _():
        conv_scr[0:CONV_HALO, :] = jnp.broadcast_to(-b_up_ref[:, 0:D_FF], (CONV_HALO, D_FF))

    shift, scale, gate = ada_ref[0, 3:4, :], ada_ref[0, 4:5, :], ada_ref[0, 5:6, :]
    h = (x_ref[0] * (1.0 + scale) + shift).astype(_BF16)

    def project(lo, n):
        conv_scr[CONV_HALO:CONV_HALO + ts, lo:lo + n] = _dot(h, w_up_ref[:, lo:lo + n])
        up_scr[:, lo:lo + n] = _dot(h, w_up_ref[:, D_FF + lo:D_FF + lo + n])

    def gated_block(lo, n):
        w = conv_w_ref[:, lo:lo + n]
        bias = conv_b_ref[:, lo:lo + n] + b_up_ref[:, lo:lo + n] * jnp.sum(w, axis=0, keepdims=True)
        act = _gelu_tanh(_causal_conv3(conv_scr, w, lo, n, ts) + bias)
        up_g = up_scr[:, lo:lo + n] + b_up_ref[:, D_FF + lo:D_FF + lo + n]
        f_scr[:, lo:lo + n] = (act * up_g).astype(_BF16)

    blocks = _col_blocks(0, D_FF)
    assert D_FF_SPLIT // COL_BLOCK <= len(blocks) - 2
    project(*blocks[0])
    for i in range(1, len(blocks)):
        project(*blocks[i])
        if i == len(blocks) - 1:
            acc_scr[...] = _dot(f_scr[:, 0:D_FF_SPLIT], w_down_ref[0:D_FF_SPLIT, :])
        gated_block(*blocks[i - 1])
    gated_block(*blocks[-1])
    conv_scr[0:CONV_HALO, :] = conv_scr[ts:ts + CONV_HALO, :]

    def down_rows(rows):
        return acc_scr[rows, :] + _dot(f_scr[rows, D_FF_SPLIT:D_FF],
                                       w_down_ref[D_FF_SPLIT:D_FF, :])

    _residual_post_norm(x_ref, o_ref, gate, ln_g_ref, ln_b_ref, down_rows)


def _ada_kernel(c_ref, w_ref, b_ref, o_ref):
    c = c_ref[...]
    c_act = (c * _sigmoid(c)).astype(_BF16)
    o_ref[0] = _dot(c_act, w_ref[0].astype(_BF16)) + b_ref[0]


def _resident_spec(array, layer):
    if layer is None:
        zeros = (0,) * array.ndim
        return pl.BlockSpec(array.shape, lambda b, s: zeros, pipeline_mode=pl.Buffered(1))
    zeros = (0,) * (array.ndim - 1)
    return pl.BlockSpec((None,) + array.shape[1:], lambda b, s: (layer,) + zeros,
                        pipeline_mode=pl.Buffered(1))


def _seq_tiled_call(body, x, ada, layer, weights, scratch_shapes, name, cast=()):
    batch, seq, d = x.shape
    assert seq % SEQ_TILE == 0 and SEQ_TILE % GMLP_BLOCK == 0 and SEQ_TILE % ROW_BLOCK == 0
    tiles = seq // SEQ_TILE
    steps = batch * tiles
    n_in, n_cast = 2 + len(weights), len(cast)

    def cast_plan(w, src_layer):
        cols = w.shape[-1]
        rows = w.size // (w.shape[0] * cols)
        view = w.reshape(w.shape[0], rows, cols)
        stay = next(k for k in range(1, steps + 1)
                    if steps % k == 0 and (rows * k) % (steps * BF16_SUBLANES) == 0)
        slab = rows * stay // steps
        src = pl.BlockSpec((None, slab, cols),
                           lambda b, s: (src_layer, (b * tiles + s) // stay, 0))
        dst = pl.BlockSpec((slab, cols), lambda b, s: ((b * tiles + s) // stay, 0))
        return view, src, dst, jax.ShapeDtypeStruct((rows, cols), _BF16)

    plans = [cast_plan(w, l) for w, l in cast]

    def kernel_body(*refs):
        ins, cast_in = refs[:n_in], refs[n_in:n_in + n_cast]
        o_ref, cast_out = refs[n_in + n_cast], refs[n_in + n_cast + 1:n_in + 2 * n_cast + 1]
        for src, dst in zip(cast_in, cast_out):
            dst[...] = src[...].astype(_BF16)
        body(*ins, o_ref, *refs[n_in + 2 * n_cast + 1:])

    x_spec = pl.BlockSpec((1, SEQ_TILE, d), lambda b, s: (b, s, 0))
    ada_spec = pl.BlockSpec((None, 1) + ada.shape[2:], lambda b, s: (layer, b, 0, 0))
    outs = pl.pallas_call(
        kernel_body,
        grid=(batch, tiles),
        in_specs=([x_spec, ada_spec] + [_resident_spec(w, l) for w, l in weights]
                  + [p[1] for p in plans]),
        out_specs=[x_spec] + [p[2] for p in plans],
        out_shape=[jax.ShapeDtypeStruct(x.shape, x.dtype)] + [p[3] for p in plans],
        scratch_shapes=scratch_shapes,
        compiler_params=pltpu.CompilerParams(
            dimension_semantics=("arbitrary", "arbitrary"),
            vmem_limit_bytes=VMEM_LIMIT_BYTES),
        name=name,
    )(x, ada, *[w for w, _ in weights], *[p[0] for p in plans])
    return outs[0], [o.reshape(w.shape[1:]) for o, (w, _) in zip(outs[1:], cast)]


def _token_mix(x, ada, layer, weights, cast):
    ts = SEQ_TILE
    scratch = [pltpu.VMEM((ts, D_MODEL), _F32) for _ in range(4)] + [
        pltpu.VMEM((CONV_HALO + ts, D_CONV), _F32),
        pltpu.VMEM((POOL_HALO + ts, D_POOL), _F32),
        pltpu.VMEM((POOL_HALO + ts, POOL_GROUP), _F32),
        pltpu.VMEM((POOL_HALO + ts, POOL_GROUP), _F32),
        pltpu.VMEM((ts, D_CONV), _BF16),
        pltpu.VMEM((ts, D_GMLP), _BF16),
        pltpu.VMEM((ts, D_MODEL), _F32),
    ]
    return _seq_tiled_call(_token_mix_kernel, x, ada, layer, weights, scratch, "token_mix", cast)


def _channel_mix(x, ada, layer, weights, cast):
    ts = SEQ_TILE
    scratch = [
        pltpu.VMEM((ts, D_FF), _F32),
        pltpu.VMEM((CONV_HALO + ts, D_FF), _F32),
        pltpu.VMEM((ts, D_FF), _BF16),
        pltpu.VMEM((ts, D_MODEL), _F32),
    ]
    return _seq_tiled_call(_channel_mix_kernel, x, ada, layer, weights, scratch, "channel_mix",
                           cast)


def _ada_modulation(c, w_ada, b_ada):
    depth, d, n_out = w_ada.shape
    batch = c.shape[0]
    rows = -(-batch // SUBLANES) * SUBLANES
    c_pad = jnp.pad(c, ((0, rows - batch), (0, 0)))
    col_block = 2 * d
    assert n_out % col_block == 0
    out = pl.pallas_call(
        _ada_kernel,
        grid=(depth, n_out // col_block),
        in_specs=[
            pl.BlockSpec((rows, d), lambda l, j: (0, 0)),
            pl.BlockSpec((1, d, col_block), lambda l, j: (l, 0, j)),
            pl.BlockSpec((1, 1, col_block), lambda l, j: (l, 0, j)),
        ],
        out_specs=pl.BlockSpec((1, rows, col_block), lambda l, j: (l, 0, j)),
        out_shape=jax.ShapeDtypeStruct((depth, rows, n_out), _F32),
        compiler_params=pltpu.CompilerParams(
            dimension_semantics=("arbitrary", "arbitrary")),
        name="ada_modulation",
    )(c_pad, w_ada, b_ada.reshape(depth, 1, n_out))
    return out[:, :batch, :].reshape(depth, batch, n_out // d, d)


def kernel(x, c, w_ada, b_ada, w_in, b_in, conv_a, w_a_out, ln_v_g, ln_v_b, w_spatial, b_spatial, w_b_out, w_pool, pool_scale, w_o, ln1_g, ln1_b, w_up, b_up, conv_ffn, conv_ffn_b, w_down, ln2_g, ln2_b):
    depth = w_in.shape[0]
    ada = _ada_modulation(c, w_ada, b_ada)

    def row(p):
        return p[:, None, :]

    b_sp = jnp.repeat(jnp.swapaxes(b_spatial, 1, 2), GMLP_GROUP, axis=2)

    def token_weights(layer, w_in_h, w_a_out_h, w_sp_h, w_b_out_h, w_pool_h, w_o_h):
        return [(w_in_h, None), (row(b_in), layer), (conv_a, layer), (w_a_out_h, None),
                (row(ln_v_g), layer), (row(ln_v_b), layer), (w_sp_h, None), (b_sp, layer),
                (w_b_out_h, None), (w_pool_h, None), (row(pool_scale), layer), (w_o_h, None),
                (row(ln1_g), layer), (row(ln1_b), layer)]

    def channel_weights(layer, w_up_h, w_down_h):
        return [(w_up_h, None), (row(b_up), layer), (conv_ffn, layer), (row(conv_ffn_b), layer),
                (w_down_h, None), (row(ln2_g), layer), (row(ln2_b), layer)]

    token_f32 = (w_in, w_a_out, w_spatial, w_b_out, w_pool, w_o)
    channel_f32 = (w_up, w_down)

    token_h = [w[0].astype(_BF16) for w in token_f32]
    for layer in range(depth):
        x, channel_h = _token_mix(x, ada, layer, token_weights(layer, *token_h),
                                  [(w, layer) for w in channel_f32])
        nxt = [(w, layer + 1) for w in token_f32] if layer + 1 < depth else []
        x, token_h = _channel_mix(x, ada, layer, channel_weights(layer, *channel_h), nxt)
    return x
```

```python
import jax
import jax.numpy as jnp
from jax.experimental import pallas as pl
from jax.experimental.pallas import tpu as pltpu

D_MODEL = 1024
DEPTH = 2
CHUNK = 64
CONV_WIDTH = 3
D_CONV = 1024
D_GMLP = 1024
GMLP_BLOCK = 128
N_GROUPS_GMLP = 8
GMLP_GROUP = D_GMLP // N_GROUPS_GMLP
D_POOL = 1024
POOL_WINDOWS = (2, 4, 8, 16)
POOL_GROUP = D_POOL // len(POOL_WINDOWS)
D_FF = 2816
D_IN = 3 * D_CONV + 2 * D_GMLP + D_POOL + 3 * D_MODEL
ALPHA = (2 * DEPTH) ** 0.25
LN_EPS = 1e-5
GELU_C = 0.7978845608028654
GELU_A = 0.044715

OFF_B, OFF_C, OFF_X = 0, D_CONV, 2 * D_CONV
OFF_U = 3 * D_CONV
OFF_V = OFF_U + D_GMLP
OFF_P = OFF_V + D_GMLP
OFF_G = OFF_P + D_POOL

SUBLANES = 8
V7X_MXU_DIM = 256
SEQ_TILE = 512
ROW_BLOCK = 256
CONV_HALO = SUBLANES
POOL_HALO = 32
COL_BLOCK = 512
ADA_COL_BLOCK = 1536
D_FF_SPLIT = 4 * COL_BLOCK
assert D_FF_SPLIT % V7X_MXU_DIM == 0 and (D_FF - D_FF_SPLIT) % V7X_MXU_DIM == 0
BF16_SUBLANES = 2 * SUBLANES
V7X_VMEM_BYTES = 64 * 1024 * 1024
VMEM_LIMIT_BYTES = V7X_VMEM_BYTES * 15 // 16

_F32 = jnp.float32
_BF16 = jnp.bfloat16


def _gelu_tanh(x):
    inner = x * (GELU_C + (GELU_C * GELU_A) * (x * x))
    half = 0.5 * x
    return half + half * jnp.tanh(inner)


def _sigmoid(x):
    return 0.5 * (jnp.tanh(0.5 * x) + 1.0)


def _layer_norm(r, g, b):
    mu = jnp.mean(r, axis=-1, keepdims=True)
    d = r - mu
    var = jnp.mean(d * d, axis=-1, keepdims=True)
    return d * jax.lax.rsqrt(var + LN_EPS) * g + b


def _dot(a, b):
    return jnp.dot(a, b, preferred_element_type=_F32)


def _col_blocks(lo, hi):
    return [(c, min(COL_BLOCK, hi - c)) for c in range(lo, hi, COL_BLOCK)]


def _causal_conv3(scr, w, lo, n, ts):
    base = CONV_HALO - (CONV_WIDTH - 1)
    acc = None
    for k in range(CONV_WIDTH):
        term = w[k:k + 1, :] * scr[base + k:base + k + ts, lo:lo + n]
        acc = term if acc is None else acc + term
    return acc


def _residual_post_norm(x_ref, o_ref, gate, ln_g_ref, ln_b_ref, sublayer_rows):
    ts = x_ref.shape[1]
    for r in range(0, ts, ROW_BLOCK):
        rows = slice(r, r + ROW_BLOCK)
        res = ALPHA * x_ref[0, rows, :] + gate * sublayer_rows(rows)
        o_ref[0, rows, :] = _layer_norm(res, ln_g_ref[...], ln_b_ref[...])


def _token_mix_kernel(x_ref, ada_ref, w_in_ref, b_in_ref, conv_a_ref, w_a_out_ref,
                      lnv_g_ref, lnv_b_ref, w_sp_ref, b_sp_ref, w_b_out_ref,
                      w_pool_ref, pscale_ref, w_o_ref, ln_g_ref, ln_b_ref,
                      o_ref,
                      s0, s1, s2, s3, conv_scr, pool_scr, tmp_a, tmp_b, a_scr, v_scr,
                      merged_scr):
    ts = x_ref.shape[1]
    seq_tile = pl.program_id(1)

    @pl.when(seq_tile == 0)
    def _():
        conv_scr[0:CONV_HALO, :] = jnp.zeros((CONV_HALO, D_CONV), _F32)
        pool_scr[0:POOL_HALO, :] = jnp.zeros((POOL_HALO, D_POOL), _F32)

    shift, scale, gate = ada_ref[0, 0:1, :], ada_ref[0, 1:2, :], ada_ref[0, 2:3, :]
    h = (x_ref[0] * (1.0 + scale) + shift).astype(_BF16)

    def project(slot, off):
        for c, n in _col_blocks(0, D_MODEL):
            slot[:, c:c + n] = _dot(h, w_in_ref[:, off + c:off + c + n])

    def z(slot, off, lo=0, n=D_MODEL):
        return slot[:, lo:lo + n] + b_in_ref[:, off + lo:off + lo + n]

    project(s0, OFF_C)
    project(s1, OFF_X)
    project(s2, OFF_B)
    conv_scr[CONV_HALO:CONV_HALO + ts, :] = z(s0, OFF_C) * z(s1, OFF_X)

    project(s3, OFF_V)
    for lo, n in _col_blocks(0, D_CONV):
        cv = _causal_conv3(conv_scr, conv_a_ref[:, lo:lo + n], lo, n, ts)
        a_scr[:, lo:lo + n] = (z(s2, OFF_B, lo, n) * cv).astype(_BF16)
    conv_scr[0:CONV_HALO, :] = conv_scr[ts:ts + CONV_HALO, :]

    project(s0, OFF_U)
    v = _layer_norm(_gelu_tanh(z(s3, OFF_V)), lnv_g_ref[...], lnv_b_ref[...])
    v_scr[...] = v.astype(_BF16)

    project(s1, OFF_G)
    s2[...] = _dot(a_scr[...], w_a_out_ref[...])

    chunk_shift = CHUNK.bit_length() - 1
    assert CHUNK == 1 << chunk_shift
    block_idx = (GMLP_BLOCK, GMLP_BLOCK)
    pos = jax.lax.broadcasted_iota(jnp.int32, block_idx, 0) >> chunk_shift
    src = jax.lax.broadcasted_iota(jnp.int32, block_idx, 1) >> chunk_shift
    allowed = src <= pos
    for g in range(N_GROUPS_GMLP):
        cols = slice(g * GMLP_GROUP, (g + 1) * GMLP_GROUP)
        w_g = jnp.where(allowed, w_sp_ref[g], jnp.zeros((), _BF16))
        n_blocks = ts // GMLP_BLOCK
        rhs = jnp.concatenate([v_scr[n * GMLP_BLOCK:(n + 1) * GMLP_BLOCK, cols]
                               for n in range(n_blocks)], axis=1)
        mixed = _dot(w_g, rhs)
        for n in range(n_blocks):
            s3[n * GMLP_BLOCK:(n + 1) * GMLP_BLOCK, cols] = (
                mixed[:, n * GMLP_BLOCK:(n + 1) * GMLP_BLOCK])

    merged_scr[...] = _sigmoid(z(s1, OFF_G)) * s2[...]
    project(s1, OFF_P)
    for n in range(ts // GMLP_BLOCK):
        rows = slice(n * GMLP_BLOCK, (n + 1) * GMLP_BLOCK)
        u = _gelu_tanh(s0[rows, :] + b_in_ref[:, OFF_U:OFF_U + D_GMLP])
        v_scr[rows, :] = (u * (s3[rows, :] + b_sp_ref[...])).astype(_BF16)

    project(s2, OFF_G + D_MODEL)
    pool_scr[POOL_HALO:POOL_HALO + ts, :] = z(s1, OFF_P)
    row = seq_tile * ts + jax.lax.broadcasted_iota(jnp.int32, (ts, 1), 0)
    ext = POOL_HALO + ts
    for k, win in enumerate(POOL_WINDOWS):
        cols = slice(k * POOL_GROUP, (k + 1) * POOL_GROUP)
        src_ref, src_cols, width = pool_scr, cols, 1
        bufs = (tmp_a, tmp_b)
        stage = 0
        while 2 * width < win:
            start = SUBLANES * (stage + 1)
            dst = bufs[stage % 2]
            dst[start:ext, :] = (src_ref[start:ext, src_cols]
                                 + src_ref[start - width:ext - width, src_cols])
            src_ref, src_cols = dst, slice(0, POOL_GROUP)
            width *= 2
            stage += 1
        wsum = (src_ref[POOL_HALO:ext, src_cols]
                + src_ref[POOL_HALO - width:ext - width, src_cols])
        denom = jnp.minimum(row + 1, win).astype(_F32)
        a_scr[:, cols] = (wsum / denom - pool_scr[POOL_HALO:ext, cols]).astype(_BF16)
    pool_scr[0:POOL_HALO, :] = pool_scr[ts:ts + POOL_HALO, :]

    s0[...] = _dot(v_scr[...], w_b_out_ref[...])
    project(s3, OFF_G + 2 * D_MODEL)
    merged_scr[...] += _sigmoid(z(s2, OFF_G + D_MODEL)) * s0[...]

    for k in range(len(POOL_WINDOWS)):
        cols = slice(k * POOL_GROUP, (k + 1) * POOL_GROUP)
        s1[:, cols] = _dot(a_scr[:, cols], w_pool_ref[k])
    merged_scr[...] += _sigmoid(z(s3, OFF_G + 2 * D_MODEL)) * (s1[...] * pscale_ref[...])

    def merge_rows(rows):
        return _dot(merged_scr[rows, :].astype(_BF16), w_o_ref[...])

    _residual_post_norm(x_ref, o_ref, gate, ln_g_ref, ln_b_ref, merge_rows)


def _channel_mix_kernel(x_ref, ada_ref, w_up_ref, b_up_ref, conv_w_ref, conv_b_ref,
                        w_down_ref, ln_g_ref, ln_b_ref, o_ref,
                        up_scr, conv_scr, f_scr, acc_scr):
    ts = x_ref.shape[1]

    @pl.when(pl.program_id(1) == 0)
    def _():
        conv_scr[0:CONV_HALO, :] = jnp.broadcast_to(-b_up_ref[:, 0:D_FF], (CONV_HALO, D_FF))

    shift, scale, gate = ada_ref[0, 3:4, :], ada_ref[0, 4:5, :], ada_ref[0, 5:6, :]
    h = (x_ref[0] * (1.0 + scale) + shift).astype(_BF16)

    def project(lo, n):
        conv_scr[CONV_HALO:CONV_HALO + ts, lo:lo + n] = _dot(h, w_up_ref[:, lo:lo + n])
        up_scr[:, lo:lo + n] = _dot(h, w_up_ref[:, D_FF + lo:D_FF + lo + n])

    def gated_block(lo, n):
        w = conv_w_ref[:, lo:lo + n]
        bias = conv_b_ref[:, lo:lo + n] + b_up_ref[:, lo:lo + n] * jnp.sum(w, axis=0, keepdims=True)
        act = _gelu_tanh(_causal_conv3(conv_scr, w, lo, n, ts) + bias)
        up_g = up_scr[:, lo:lo + n] + b_up_ref[:, D_FF + lo:D_FF + lo + n]
        f_scr[:, lo:lo + n] = (act * up_g).astype(_BF16)

    blocks = _col_blocks(0, D_FF)
    assert D_FF_SPLIT // COL_BLOCK <= len(blocks) - 2
    project(*blocks[0])
    for i in range(1, len(blocks)):
        project(*blocks[i])
        if i == len(blocks) - 1:
            acc_scr[...] = _dot(f_scr[:, 0:D_FF_SPLIT], w_down_ref[0:D_FF_SPLIT, :])
        gated_block(*blocks[i - 1])
    gated_block(*blocks[-1])
    conv_scr[0:CONV_HALO, :] = conv_scr[ts:ts + CONV_HALO, :]

    def down_rows(rows):
        return acc_scr[rows, :] + _dot(f_scr[rows, D_FF_SPLIT:D_FF],
                                       w_down_ref[D_FF_SPLIT:D_FF, :])

    _residual_post_norm(x_ref, o_ref, gate, ln_g_ref, ln_b_ref, down_rows)


def _ada_kernel(c_ref, w_ref, b_ref, o_ref):
    c = c_ref[...]
    c_act = (c * _sigmoid(c)).astype(_BF16)
    o_ref[0] = _dot(c_act, w_ref[0].astype(_BF16)) + b_ref[0]


def _resident_spec(array, layer):
    if layer is None:
        zeros = (0,) * array.ndim
        return pl.BlockSpec(array.shape, lambda b, s: zeros, pipeline_mode=pl.Buffered(1))
    zeros = (0,) * (array.ndim - 1)
    return pl.BlockSpec((None,) + array.shape[1:], lambda b, s: (layer,) + zeros,
                        pipeline_mode=pl.Buffered(1))


def _call_with_casts(body, grid, in_specs, out_spec, out_shape, operands, cast, name,
                     scratch_shapes=()):
    inner = grid[1]
    steps = grid[0] * inner
    n_in, n_cast = len(operands), len(cast)

    def cast_plan(w, src_layer):
        cols = w.shape[-1]
        rows = w.size // (w.shape[0] * cols)
        view = w.reshape(w.shape[0], rows, cols)
        stay = next(k for k in range(1, steps + 1)
                    if steps % k == 0 and (rows * k) % (steps * BF16_SUBLANES) == 0)
        slab = rows * stay // steps
        src = pl.BlockSpec((None, slab, cols),
                           lambda i, j: (src_layer, (i * inner + j) // stay, 0))
        dst = pl.BlockSpec((slab, cols), lambda i, j: ((i * inner + j) // stay, 0))
        return view, src, dst, jax.ShapeDtypeStruct((rows, cols), _BF16)

    plans = [cast_plan(w, l) for w, l in cast]

    def kernel_body(*refs):
        ins, cast_in = refs[:n_in], refs[n_in:n_in + n_cast]
        o_ref, cast_out = refs[n_in + n_cast], refs[n_in + n_cast + 1:n_in + 2 * n_cast + 1]
        for src, dst in zip(cast_in, cast_out):
            dst[...] = src[...].astype(_BF16)
        body(*ins, o_ref, *refs[n_in + 2 * n_cast + 1:])

    outs = pl.pallas_call(
        kernel_body,
        grid=grid,
        in_specs=list(in_specs) + [p[1] for p in plans],
        out_specs=[out_spec] + [p[2] for p in plans],
        out_shape=[out_shape] + [p[3] for p in plans],
        scratch_shapes=scratch_shapes,
        compiler_params=pltpu.CompilerParams(
            dimension_semantics=("arbitrary", "arbitrary"),
            vmem_limit_bytes=VMEM_LIMIT_BYTES),
        name=name,
    )(*operands, *[p[0] for p in plans])
    return outs[0], [o.reshape(w.shape[1:]) for o, (w, _) in zip(outs[1:], cast)]


def _seq_tiled_call(body, x, ada, layer, weights, scratch_shapes, name, cast=()):
    batch, seq, d = x.shape
    assert seq % SEQ_TILE == 0 and SEQ_TILE % GMLP_BLOCK == 0 and SEQ_TILE % ROW_BLOCK == 0
    x_spec = pl.BlockSpec((1, SEQ_TILE, d), lambda b, s: (b, s, 0))
    ada_spec = pl.BlockSpec((None, 1) + ada.shape[2:], lambda b, s: (layer, b, 0, 0))
    return _call_with_casts(
        body, (batch, seq // SEQ_TILE),
        [x_spec, ada_spec] + [_resident_spec(w, l) for w, l in weights],
        x_spec, jax.ShapeDtypeStruct(x.shape, x.dtype),
        [x, ada] + [w for w, _ in weights], cast, name, scratch_shapes)


def _token_mix(x, ada, layer, weights, cast):
    ts = SEQ_TILE
    scratch = [pltpu.VMEM((ts, D_MODEL), _F32) for _ in range(4)] + [
        pltpu.VMEM((CONV_HALO + ts, D_CONV), _F32),
        pltpu.VMEM((POOL_HALO + ts, D_POOL), _F32),
        pltpu.VMEM((POOL_HALO + ts, POOL_GROUP), _F32),
        pltpu.VMEM((POOL_HALO + ts, POOL_GROUP), _F32),
        pltpu.VMEM((ts, D_CONV), _BF16),
        pltpu.VMEM((ts, D_GMLP), _BF16),
        pltpu.VMEM((ts, D_MODEL), _F32),
    ]
    return _seq_tiled_call(_token_mix_kernel, x, ada, layer, weights, scratch, "token_mix", cast)


def _channel_mix(x, ada, layer, weights, cast):
    ts = SEQ_TILE
    scratch = [
        pltpu.VMEM((ts, D_FF), _F32),
        pltpu.VMEM((CONV_HALO + ts, D_FF), _F32),
        pltpu.VMEM((ts, D_FF), _BF16),
        pltpu.VMEM((ts, D_MODEL), _F32),
    ]
    return _seq_tiled_call(_channel_mix_kernel, x, ada, layer, weights, scratch, "channel_mix",
                           cast)


def _ada_modulation(c, w_ada, b_ada, cast):
    depth, d, n_out = w_ada.shape
    batch = c.shape[0]
    rows = -(-batch // SUBLANES) * SUBLANES
    c_pad = jnp.pad(c, ((0, rows - batch), (0, 0)))
    col_block = ADA_COL_BLOCK
    assert n_out % col_block == 0
    out, cast_h = _call_with_casts(
        _ada_kernel, (depth, n_out // col_block),
        [pl.BlockSpec((rows, d), lambda l, j: (0, 0)),
         pl.BlockSpec((1, d, col_block), lambda l, j: (l, 0, j)),
         pl.BlockSpec((1, 1, col_block), lambda l, j: (l, 0, j))],
        pl.BlockSpec((1, rows, col_block), lambda l, j: (l, 0, j)),
        jax.ShapeDtypeStruct((depth, rows, n_out), _F32),
        [c_pad, w_ada, b_ada.reshape(depth, 1, n_out)], cast, "ada_modulation")
    return out[:, :batch, :].reshape(depth, batch, n_out // d, d), cast_h


def kernel(x, c, w_ada, b_ada, w_in, b_in, conv_a, w_a_out, ln_v_g, ln_v_b, w_spatial, b_spatial, w_b_out, w_pool, pool_scale, w_o, ln1_g, ln1_b, w_up, b_up, conv_ffn, conv_ffn_b, w_down, ln2_g, ln2_b):
    depth = w_in.shape[0]
    token_f32 = (w_in, w_a_out, w_spatial, w_b_out, w_pool, w_o)
    channel_f32 = (w_up, w_down)
    ada, token_h = _ada_modulation(c, w_ada, b_ada, [(w, 0) for w in token_f32])

    def row(p):
        return p[:, None, :]

    b_sp = jnp.repeat(jnp.swapaxes(b_spatial, 1, 2), GMLP_GROUP, axis=2)

    def token_weights(layer, w_in_h, w_a_out_h, w_sp_h, w_b_out_h, w_pool_h, w_o_h):
        return [(w_in_h, None), (row(b_in), layer), (conv_a, layer), (w_a_out_h, None),
                (row(ln_v_g), layer), (row(ln_v_b), layer), (w_sp_h, None), (b_sp, layer),
                (w_b_out_h, None), (w_pool_h, None), (row(pool_scale), layer), (w_o_h, None),
                (row(ln1_g), layer), (row(ln1_b), layer)]

    def channel_weights(layer, w_up_h, w_down_h):
        return [(w_up_h, None), (row(b_up), layer), (conv_ffn, layer), (row(conv_ffn_b), layer),
                (w_down_h, None), (row(ln2_g), layer), (row(ln2_b), layer)]

    for layer in range(depth):
        x, channel_h = _token_mix(x, ada, layer, token_weights(layer, *token_h),
                                  [(w, layer) for w in channel_f32])
        nxt = [(w, layer + 1) for w in token_f32] if layer + 1 < depth else []
        x, token_h = _channel_mix(x, ada, layer, channel_weights(layer, *channel_h), nxt)
    return x
```

```python
import jax
import jax.numpy as jnp
from jax.experimental import pallas as pl
from jax.experimental.pallas import tpu as pltpu

D_MODEL = 1024
DEPTH = 2
CHUNK = 64
CONV_WIDTH = 3
D_CONV = 1024
D_GMLP = 1024
GMLP_BLOCK = 128
N_GROUPS_GMLP = 8
GMLP_GROUP = D_GMLP // N_GROUPS_GMLP
D_POOL = 1024
POOL_WINDOWS = (2, 4, 8, 16)
POOL_GROUP = D_POOL // len(POOL_WINDOWS)
D_FF = 2816
D_IN = 3 * D_CONV + 2 * D_GMLP + D_POOL + 3 * D_MODEL
ALPHA = (2 * DEPTH) ** 0.25
LN_EPS = 1e-5
GELU_C = 0.7978845608028654
GELU_A = 0.044715

OFF_B, OFF_C, OFF_X = 0, D_CONV, 2 * D_CONV
OFF_U = 3 * D_CONV
OFF_V = OFF_U + D_GMLP
OFF_P = OFF_V + D_GMLP
OFF_G = OFF_P + D_POOL

SUBLANES = 8
V7X_MXU_DIM = 256
SEQ_TILE = 512
ROW_BLOCK = 256
CONV_HALO = SUBLANES
POOL_HALO = 32
COL_BLOCK = 512
ADA_COL_BLOCK = 1536
D_FF_SPLIT = 4 * COL_BLOCK
assert D_FF_SPLIT % V7X_MXU_DIM == 0 and (D_FF - D_FF_SPLIT) % V7X_MXU_DIM == 0
BF16_SUBLANES = 2 * SUBLANES
V7X_VMEM_BYTES = 64 * 1024 * 1024
VMEM_LIMIT_BYTES = V7X_VMEM_BYTES * 15 // 16

_F32 = jnp.float32
_BF16 = jnp.bfloat16


def _gelu_tanh(x):
    inner = x * (GELU_C + (GELU_C * GELU_A) * (x * x))
    half = 0.5 * x
    return half + half * jnp.tanh(inner)


def _sigmoid(x):
    return 0.5 * (jnp.tanh(0.5 * x) + 1.0)


def _layer_norm(r, g, b):
    mu = jnp.mean(r, axis=-1, keepdims=True)
    d = r - mu
    var = jnp.mean(d * d, axis=-1, keepdims=True)
    return d * jax.lax.rsqrt(var + LN_EPS) * g + b


def _dot(a, b):
    return jnp.dot(a, b, preferred_element_type=_F32)


def _col_blocks(lo, hi):
    return [(c, min(COL_BLOCK, hi - c)) for c in range(lo, hi, COL_BLOCK)]


def _causal_conv3(scr, w, lo, n, ts):
    base = CONV_HALO - (CONV_WIDTH - 1)
    acc = None
    for k in range(CONV_WIDTH):
        term = w[k:k + 1, :] * scr[base + k:base + k + ts, lo:lo + n]
        acc = term if acc is None else acc + term
    return acc


def _residual_post_norm(x_ref, o_ref, gate, ln_g_ref, ln_b_ref, sublayer_rows):
    ts = x_ref.shape[1]
    for r in range(0, ts, ROW_BLOCK):
        rows = slice(r, r + ROW_BLOCK)
        res = ALPHA * x_ref[0, rows, :] + gate * sublayer_rows(rows)
        o_ref[0, rows, :] = _layer_norm(res, ln_g_ref[...], ln_b_ref[...])


def _token_mix_kernel(x_ref, ada_ref, w_in_ref, b_in_ref, conv_a_ref, w_a_out_ref,
                      lnv_g_ref, lnv_b_ref, w_sp_ref, b_sp_ref, w_b_out_ref,
                      w_pool_ref, pscale_ref, w_o_ref, ln_g_ref, ln_b_ref,
                      o_ref,
                      s0, s1, s2, s3, conv_scr, pool_scr, tmp_a, tmp_b, a_scr, v_scr,
                      merged_scr):
    ts = x_ref.shape[1]
    seq_tile = pl.program_id(1)

    @pl.when(seq_tile == 0)
    def _():
        conv_scr[0:CONV_HALO, :] = jnp.zeros((CONV_HALO, D_CONV), _F32)
        pool_scr[0:POOL_HALO, :] = jnp.zeros((POOL_HALO, D_POOL), _F32)

    shift, scale, gate = ada_ref[0, 0:1, :], ada_ref[0, 1:2, :], ada_ref[0, 2:3, :]
    h = (x_ref[0] * (1.0 + scale) + shift).astype(_BF16)

    def project(slot, off):
        for c, n in _col_blocks(0, D_MODEL):
            slot[:, c:c + n] = _dot(h, w_in_ref[:, off + c:off + c + n])

    def z(slot, off, lo=0, n=D_MODEL):
        return slot[:, lo:lo + n] + b_in_ref[:, off + lo:off + lo + n]

    project(s0, OFF_C)
    project(s1, OFF_X)
    project(s2, OFF_B)
    conv_scr[CONV_HALO:CONV_HALO + ts, :] = z(s0, OFF_C) * z(s1, OFF_X)

    project(s3, OFF_V)
    for lo, n in _col_blocks(0, D_CONV):
        cv = _causal_conv3(conv_scr, conv_a_ref[:, lo:lo + n], lo, n, ts)
        a_scr[:, lo:lo + n] = (z(s2, OFF_B, lo, n) * cv).astype(_BF16)
    conv_scr[0:CONV_HALO, :] = conv_scr[ts:ts + CONV_HALO, :]

    project(s0, OFF_U)
    v = _layer_norm(_gelu_tanh(z(s3, OFF_V)), lnv_g_ref[...], lnv_b_ref[...])
    v_scr[...] = v.astype(_BF16)

    project(s1, OFF_G)
    s2[...] = _dot(a_scr[...], w_a_out_ref[...])

    chunk_shift = CHUNK.bit_length() - 1
    assert CHUNK == 1 << chunk_shift
    block_idx = (GMLP_BLOCK, GMLP_BLOCK)
    pos = jax.lax.broadcasted_iota(jnp.int32, block_idx, 0) >> chunk_shift
    src = jax.lax.broadcasted_iota(jnp.int32, block_idx, 1) >> chunk_shift
    allowed = src <= pos
    for g in range(N_GROUPS_GMLP):
        cols = slice(g * GMLP_GROUP, (g + 1) * GMLP_GROUP)
        w_g = jnp.where(allowed, w_sp_ref[g], jnp.zeros((), _BF16))
        n_blocks = ts // GMLP_BLOCK
        rhs = jnp.concatenate([v_scr[n * GMLP_BLOCK:(n + 1) * GMLP_BLOCK, cols]
                               for n in range(n_blocks)], axis=1)
        mixed = _dot(w_g, rhs)
        for n in range(n_blocks):
            s3[n * GMLP_BLOCK:(n + 1) * GMLP_BLOCK, cols] = (
                mixed[:, n * GMLP_BLOCK:(n + 1) * GMLP_BLOCK])

    merged_scr[...] = _sigmoid(z(s1, OFF_G)) * s2[...]
    project(s1, OFF_P)
    for n in range(ts // GMLP_BLOCK):
        rows = slice(n * GMLP_BLOCK, (n + 1) * GMLP_BLOCK)
        u = _gelu_tanh(s0[rows, :] + b_in_ref[:, OFF_U:OFF_U + D_GMLP])
        v_scr[rows, :] = (u * (s3[rows, :] + b_sp_ref[...])).astype(_BF16)

    project(s2, OFF_G + D_MODEL)
    pool_scr[POOL_HALO:POOL_HALO + ts, :] = z(s1, OFF_P)
    row = seq_tile * ts + jax.lax.broadcasted_iota(jnp.int32, (ts, 1), 0)
    ext = POOL_HALO + ts
    for k, win in enumerate(POOL_WINDOWS):
        cols = slice(k * POOL_GROUP, (k + 1) * POOL_GROUP)
        src_ref, src_cols, width = pool_scr, cols, 1
        bufs = (tmp_a, tmp_b)
        stage = 0
        while 2 * width < win:
            start = SUBLANES * (stage + 1)
            dst = bufs[stage % 2]
            dst[start:ext, :] = (src_ref[start:ext, src_cols]
                                 + src_ref[start - width:ext - width, src_cols])
            src_ref, src_cols = dst, slice(0, POOL_GROUP)
            width *= 2
            stage += 1
        wsum = (src_ref[POOL_HALO:ext, src_cols]
                + src_ref[POOL_HALO - width:ext - width, src_cols])
        denom = jnp.minimum(row + 1, win).astype(_F32)
        a_scr[:, cols] = (wsum / denom - pool_scr[POOL_HALO:ext, cols]).astype(_BF16)
    pool_scr[0:POOL_HALO, :] = pool_scr[ts:ts + POOL_HALO, :]

    s0[...] = _dot(v_scr[...], w_b_out_ref[...])
    project(s3, OFF_G + 2 * D_MODEL)
    merged_scr[...] += _sigmoid(z(s2, OFF_G + D_MODEL)) * s0[...]

    for k in range(len(POOL_WINDOWS)):
        cols = slice(k * POOL_GROUP, (k + 1) * POOL_GROUP)
        s1[:, cols] = _dot(a_scr[:, cols], w_pool_ref[k])
    merged_scr[...] += _sigmoid(z(s3, OFF_G + 2 * D_MODEL)) * (s1[...] * pscale_ref[...])

    def merge_rows(rows):
        return _dot(merged_scr[rows, :].astype(_BF16), w_o_ref[...])

    _residual_post_norm(x_ref, o_ref, gate, ln_g_ref, ln_b_ref, merge_rows)


def _channel_mix_kernel(x_ref, ada_ref, w_up_ref, b_up_ref, conv_w_ref, conv_b_ref,
                        w_down_ref, ln_g_ref, ln_b_ref, o_ref,
                        up_scr, conv_scr, f_scr, acc_scr):
    ts = x_ref.shape[1]

    @pl.when(pl.program_id(1) == 0)
    def _():
        conv_scr[0:CONV_HALO, :] = jnp.broadcast_to(-b_up_ref[:, 0:D_FF], (CONV_HALO, D_FF))

    shift, scale, gate = ada_ref[0, 3:4, :], ada_ref[0, 4:5, :], ada_ref[0, 5:6, :]
    h = (x_ref[0] * (1.0 + scale) + shift).astype(_BF16)

    def project(lo, n):
        conv_scr[CONV_HALO:CONV_HALO + ts, lo:lo + n] = _dot(h, w_up_ref[:, lo:lo + n])
        up_scr[:, lo:lo + n] = _dot(h, w_up_ref[:, D_FF + lo:D_FF + lo + n])

    def gated_block(lo, n):
        w = conv_w_ref[:, lo:lo + n]
        bias = conv_b_ref[:, lo:lo + n] + b_up_ref[:, lo:lo + n] * jnp.sum(w, axis=0, keepdims=True)
        act = _gelu_tanh(_causal_conv3(conv_scr, w, lo, n, ts) + bias)
        up_g = up_scr[:, lo:lo + n] + b_up_ref[:, D_FF + lo:D_FF + lo + n]
        f_scr[:, lo:lo + n] = (act * up_g).astype(_BF16)

    blocks = _col_blocks(0, D_FF)
    assert D_FF_SPLIT // COL_BLOCK <= len(blocks) - 2
    project(*blocks[0])
    for i in range(1, len(blocks)):
        project(*blocks[i])
        if i == len(blocks) - 1:
            acc_scr[...] = _dot(f_scr[:, 0:D_FF_SPLIT], w_down_ref[0:D_FF_SPLIT, :])
        gated_block(*blocks[i - 1])
    gated_block(*blocks[-1])
    conv_scr[0:CONV_HALO, :] = conv_scr[ts:ts + CONV_HALO, :]

    def down_rows(rows):
        return acc_scr[rows, :] + _dot(f_scr[rows, D_FF_SPLIT:D_FF],
                                       w_down_ref[D_FF_SPLIT:D_FF, :])

    _residual_post_norm(x_ref, o_ref, gate, ln_g_ref, ln_b_ref, down_rows)


def _ada_kernel(c_ref, w_ref, b_ref, o_ref):
    c = c_ref[...]
    c_act = (c * _sigmoid(c)).astype(_BF16)
    o_ref[0] = _dot(c_act, w_ref[0].astype(_BF16)) + b_ref[0]


def _is_row_stack(array, layer):
    return layer is not None and array.ndim == 2


def _resident_spec(array, layer):
    if layer is None or _is_row_stack(array, layer):
        zeros = (0,) * array.ndim
        return pl.BlockSpec(array.shape, lambda b, s: zeros, pipeline_mode=pl.Buffered(1))
    zeros = (0,) * (array.ndim - 1)
    return pl.BlockSpec((None,) + array.shape[1:], lambda b, s: (layer,) + zeros,
                        pipeline_mode=pl.Buffered(1))


def _call_with_casts(body, grid, in_specs, out_spec, out_shape, operands, cast, name,
                     scratch_shapes=()):
    inner = grid[1]
    steps = grid[0] * inner
    n_in, n_cast = len(operands), len(cast)

    def cast_plan(w, src_layer):
        cols = w.shape[-1]
        rows = w.size // (w.shape[0] * cols)
        view = w.reshape(w.shape[0], rows, cols)
        stay = next(k for k in range(1, steps + 1)
                    if steps % k == 0 and (rows * k) % (steps * BF16_SUBLANES) == 0)
        slab = rows * stay // steps
        src = pl.BlockSpec((None, slab, cols),
                           lambda i, j: (src_layer, (i * inner + j) // stay, 0))
        dst = pl.BlockSpec((slab, cols), lambda i, j: ((i * inner + j) // stay, 0))
        return view, src, dst, jax.ShapeDtypeStruct((rows, cols), _BF16)

    plans = [cast_plan(w, l) for w, l in cast]

    def kernel_body(*refs):
        ins, cast_in = refs[:n_in], refs[n_in:n_in + n_cast]
        o_ref, cast_out = refs[n_in + n_cast], refs[n_in + n_cast + 1:n_in + 2 * n_cast + 1]
        for src, dst in zip(cast_in, cast_out):
            dst[...] = src[...].astype(_BF16)
        body(*ins, o_ref, *refs[n_in + 2 * n_cast + 1:])

    outs = pl.pallas_call(
        kernel_body,
        grid=grid,
        in_specs=list(in_specs) + [p[1] for p in plans],
        out_specs=[out_spec] + [p[2] for p in plans],
        out_shape=[out_shape] + [p[3] for p in plans],
        scratch_shapes=scratch_shapes,
        compiler_params=pltpu.CompilerParams(
            dimension_semantics=("arbitrary", "arbitrary"),
            vmem_limit_bytes=VMEM_LIMIT_BYTES),
        name=name,
    )(*operands, *[p[0] for p in plans])
    return outs[0], [o.reshape(w.shape[1:]) for o, (w, _) in zip(outs[1:], cast)]


def _seq_tiled_call(body, x, ada, layer, weights, scratch_shapes, name, cast=()):
    batch, seq, d = x.shape
    assert seq % SEQ_TILE == 0 and SEQ_TILE % GMLP_BLOCK == 0 and SEQ_TILE % ROW_BLOCK == 0
    x_spec = pl.BlockSpec((1, SEQ_TILE, d), lambda b, s: (b, s, 0))
    ada_spec = pl.BlockSpec((None, 1) + ada.shape[2:], lambda b, s: (layer, b, 0, 0))
    row_stacks = [2 + i for i, (w, l) in enumerate(weights) if _is_row_stack(w, l)]

    def layer_body(*refs):
        refs = list(refs)
        for i in row_stacks:
            refs[i] = refs[i].at[layer:layer + 1]
        body(*refs)

    return _call_with_casts(
        layer_body, (batch, seq // SEQ_TILE),
        [x_spec, ada_spec] + [_resident_spec(w, l) for w, l in weights],
        x_spec, jax.ShapeDtypeStruct(x.shape, x.dtype),
        [x, ada] + [w for w, _ in weights], cast, name, scratch_shapes)


def _token_mix(x, ada, layer, weights, cast):
    ts = SEQ_TILE
    scratch = [pltpu.VMEM((ts, D_MODEL), _F32) for _ in range(4)] + [
        pltpu.VMEM((CONV_HALO + ts, D_CONV), _F32),
        pltpu.VMEM((POOL_HALO + ts, D_POOL), _F32),
        pltpu.VMEM((POOL_HALO + ts, POOL_GROUP), _F32),
        pltpu.VMEM((POOL_HALO + ts, POOL_GROUP), _F32),
        pltpu.VMEM((ts, D_CONV), _BF16),
        pltpu.VMEM((ts, D_GMLP), _BF16),
        pltpu.VMEM((ts, D_MODEL), _F32),
    ]
    return _seq_tiled_call(_token_mix_kernel, x, ada, layer, weights, scratch, "token_mix", cast)


def _channel_mix(x, ada, layer, weights, cast):
    ts = SEQ_TILE
    scratch = [
        pltpu.VMEM((ts, D_FF), _F32),
        pltpu.VMEM((CONV_HALO + ts, D_FF), _F32),
        pltpu.VMEM((ts, D_FF), _BF16),
        pltpu.VMEM((ts, D_MODEL), _F32),
    ]
    return _seq_tiled_call(_channel_mix_kernel, x, ada, layer, weights, scratch, "channel_mix",
                           cast)


def _ada_modulation(c, w_ada, b_ada, cast):
    depth, d, n_out = w_ada.shape
    batch = c.shape[0]
    rows = -(-batch // SUBLANES) * SUBLANES
    c_pad = jnp.pad(c, ((0, rows - batch), (0, 0)))
    col_block = ADA_COL_BLOCK
    assert n_out % col_block == 0
    out, cast_h = _call_with_casts(
        _ada_kernel, (depth, n_out // col_block),
        [pl.BlockSpec((rows, d), lambda l, j: (0, 0)),
         pl.BlockSpec((1, d, col_block), lambda l, j: (l, 0, j)),
         pl.BlockSpec((1, 1, col_block), lambda l, j: (l, 0, j))],
        pl.BlockSpec((1, rows, col_block), lambda l, j: (l, 0, j)),
        jax.ShapeDtypeStruct((depth, rows, n_out), _F32),
        [c_pad, w_ada, b_ada.reshape(depth, 1, n_out)], cast, "ada_modulation")
    return out[:, :batch, :].reshape(depth, batch, n_out // d, d), cast_h


def kernel(x, c, w_ada, b_ada, w_in, b_in, conv_a, w_a_out, ln_v_g, ln_v_b, w_spatial, b_spatial, w_b_out, w_pool, pool_scale, w_o, ln1_g, ln1_b, w_up, b_up, conv_ffn, conv_ffn_b, w_down, ln2_g, ln2_b):
    depth = w_in.shape[0]
    token_f32 = (w_in, w_a_out, w_spatial, w_b_out, w_pool, w_o)
    channel_f32 = (w_up, w_down)
    ada, token_h = _ada_modulation(c, w_ada, b_ada, [(w, 0) for w in token_f32])

    b_sp = jnp.repeat(jnp.swapaxes(b_spatial, 1, 2), GMLP_GROUP, axis=2)

    def token_weights(layer, w_in_h, w_a_out_h, w_sp_h, w_b_out_h, w_pool_h, w_o_h):
        return [(w_in_h, None), (b_in, layer), (conv_a, layer), (w_a_out_h, None),
                (ln_v_g, layer), (ln_v_b, layer), (w_sp_h, None), (b_sp, layer),
                (w_b_out_h, None), (w_pool_h, None), (pool_scale, layer), (w_o_h, None),
                (ln1_g, layer), (ln1_b, layer)]

    def channel_weights(layer, w_up_h, w_down_h):
        return [(w_up_h, None), (b_up, layer), (conv_ffn, layer), (conv_ffn_b, layer),
                (w_down_h, None), (ln2_g, layer), (ln2_b, layer)]

    for layer in range(depth):
        x, channel_h = _token_mix(x, ada, layer, token_weights(layer, *token_h),
                                  [(w, layer) for w in channel_f32])
        nxt = [(w, layer + 1) for w in token_f32] if layer + 1 < depth else []
        x, token_h = _channel_mix(x, ada, layer, channel_weights(layer, *channel_h), nxt)
    return x
```

```python
import jax
import jax.numpy as jnp
from jax.experimental import pallas as pl
from jax.experimental.pallas import tpu as pltpu

D_MODEL = 1024
DEPTH = 2
CHUNK = 64
CONV_WIDTH = 3
D_CONV = 1024
D_GMLP = 1024
GMLP_BLOCK = 128
N_GROUPS_GMLP = 8
GMLP_GROUP = D_GMLP // N_GROUPS_GMLP
D_POOL = 1024
POOL_WINDOWS = (2, 4, 8, 16)
POOL_GROUP = D_POOL // len(POOL_WINDOWS)
D_FF = 2816
D_IN = 3 * D_CONV + 2 * D_GMLP + D_POOL + 3 * D_MODEL
ALPHA = (2 * DEPTH) ** 0.25
LN_EPS = 1e-5
GELU_C = 0.7978845608028654
GELU_A = 0.044715

OFF_B, OFF_C, OFF_X = 0, D_CONV, 2 * D_CONV
OFF_U = 3 * D_CONV
OFF_V = OFF_U + D_GMLP
OFF_P = OFF_V + D_GMLP
OFF_G = OFF_P + D_POOL

SUBLANES = 8
V7X_MXU_DIM = 256
SEQ_TILE = 512
ROW_BLOCK = 256
CONV_HALO = SUBLANES
POOL_HALO = 32
COL_BLOCK = 512
ADA_COL_BLOCK = 1536
D_FF_SPLIT = 4 * COL_BLOCK
assert D_FF_SPLIT % V7X_MXU_DIM == 0 and (D_FF - D_FF_SPLIT) % V7X_MXU_DIM == 0
BF16_SUBLANES = 2 * SUBLANES
V7X_VMEM_BYTES = 64 * 1024 * 1024
VMEM_LIMIT_BYTES = V7X_VMEM_BYTES * 15 // 16

_F32 = jnp.float32
_BF16 = jnp.bfloat16


def _gelu_tanh(x):
    inner = x * (GELU_C + (GELU_C * GELU_A) * (x * x))
    half = 0.5 * x
    return half + half * jnp.tanh(inner)


def _sigmoid(x):
    return 0.5 * (jnp.tanh(0.5 * x) + 1.0)


def _layer_norm(r, g, b):
    mu = jnp.mean(r, axis=-1, keepdims=True)
    d = r - mu
    var = jnp.mean(d * d, axis=-1, keepdims=True)
    return d * jax.lax.rsqrt(var + LN_EPS) * g + b


def _dot(a, b):
    return jnp.dot(a, b, preferred_element_type=_F32)


def _col_blocks(lo, hi):
    return [(c, min(COL_BLOCK, hi - c)) for c in range(lo, hi, COL_BLOCK)]


def _causal_conv3(scr, w, lo, n, ts):
    base = CONV_HALO - (CONV_WIDTH - 1)
    acc = None
    for k in range(CONV_WIDTH):
        term = w[k:k + 1, :] * scr[base + k:base + k + ts, lo:lo + n]
        acc = term if acc is None else acc + term
    return acc


def _ada_rows(ada_ref, first):
    batch_row = pl.ds(pl.program_id(0), 1)
    return [ada_ref[batch_row, k * D_MODEL:(k + 1) * D_MODEL] for k in range(first, first + 3)]


def _residual_post_norm(x_ref, o_ref, gate, ln_g_ref, ln_b_ref, sublayer_rows):
    ts = x_ref.shape[1]
    for r in range(0, ts, ROW_BLOCK):
        rows = slice(r, r + ROW_BLOCK)
        res = ALPHA * x_ref[0, rows, :] + gate * sublayer_rows(rows)
        o_ref[0, rows, :] = _layer_norm(res, ln_g_ref[...], ln_b_ref[...])


def _token_mix_kernel(x_ref, ada_ref, w_in_ref, b_in_ref, conv_a_ref, w_a_out_ref,
                      lnv_g_ref, lnv_b_ref, w_sp_ref, b_sp_ref, w_b_out_ref,
                      w_pool_ref, pscale_ref, w_o_ref, ln_g_ref, ln_b_ref,
                      o_ref,
                      s0, s1, s2, s3, conv_scr, pool_scr, tmp_a, tmp_b, a_scr, v_scr,
                      merged_scr):
    ts = x_ref.shape[1]
    seq_tile = pl.program_id(1)

    @pl.when(seq_tile == 0)
    def _():
        conv_scr[0:CONV_HALO, :] = jnp.zeros((CONV_HALO, D_CONV), _F32)
        pool_scr[0:POOL_HALO, :] = jnp.zeros((POOL_HALO, D_POOL), _F32)

    shift, scale, gate = _ada_rows(ada_ref, 0)
    h = (x_ref[0] * (1.0 + scale) + shift).astype(_BF16)

    def project(slot, off):
        for c, n in _col_blocks(0, D_MODEL):
            slot[:, c:c + n] = _dot(h, w_in_ref[:, off + c:off + c + n])

    def z(slot, off, lo=0, n=D_MODEL):
        return slot[:, lo:lo + n] + b_in_ref[:, off + lo:off + lo + n]

    project(s0, OFF_C)
    project(s1, OFF_X)
    project(s2, OFF_B)
    conv_scr[CONV_HALO:CONV_HALO + ts, :] = z(s0, OFF_C) * z(s1, OFF_X)

    project(s3, OFF_V)
    for lo, n in _col_blocks(0, D_CONV):
        cv = _causal_conv3(conv_scr, conv_a_ref[:, lo:lo + n], lo, n, ts)
        a_scr[:, lo:lo + n] = (z(s2, OFF_B, lo, n) * cv).astype(_BF16)
    conv_scr[0:CONV_HALO, :] = conv_scr[ts:ts + CONV_HALO, :]

    project(s0, OFF_U)
    v = _layer_norm(_gelu_tanh(z(s3, OFF_V)), lnv_g_ref[...], lnv_b_ref[...])
    v_scr[...] = v.astype(_BF16)

    project(s1, OFF_G)
    s2[...] = _dot(a_scr[...], w_a_out_ref[...])

    chunk_shift = CHUNK.bit_length() - 1
    assert CHUNK == 1 << chunk_shift
    block_idx = (GMLP_BLOCK, GMLP_BLOCK)
    pos = jax.lax.broadcasted_iota(jnp.int32, block_idx, 0) >> chunk_shift
    src = jax.lax.broadcasted_iota(jnp.int32, block_idx, 1) >> chunk_shift
    allowed = src <= pos
    for g in range(N_GROUPS_GMLP):
        cols = slice(g * GMLP_GROUP, (g + 1) * GMLP_GROUP)
        w_g = jnp.where(allowed, w_sp_ref[g], jnp.zeros((), _BF16))
        n_blocks = ts // GMLP_BLOCK
        rhs = jnp.concatenate([v_scr[n * GMLP_BLOCK:(n + 1) * GMLP_BLOCK, cols]
                               for n in range(n_blocks)], axis=1)
        mixed = _dot(w_g, rhs)
        for n in range(n_blocks):
            s3[n * GMLP_BLOCK:(n + 1) * GMLP_BLOCK, cols] = (
                mixed[:, n * GMLP_BLOCK:(n + 1) * GMLP_BLOCK])

    merged_scr[...] = _sigmoid(z(s1, OFF_G)) * s2[...]
    project(s1, OFF_P)
    for n in range(ts // GMLP_BLOCK):
        rows = slice(n * GMLP_BLOCK, (n + 1) * GMLP_BLOCK)
        u = _gelu_tanh(s0[rows, :] + b_in_ref[:, OFF_U:OFF_U + D_GMLP])
        v_scr[rows, :] = (u * (s3[rows, :] + b_sp_ref[...])).astype(_BF16)

    project(s2, OFF_G + D_MODEL)
    pool_scr[POOL_HALO:POOL_HALO + ts, :] = z(s1, OFF_P)
    row = seq_tile * ts + jax.lax.broadcasted_iota(jnp.int32, (ts, 1), 0)
    ext = POOL_HALO + ts
    for k, win in enumerate(POOL_WINDOWS):
        cols = slice(k * POOL_GROUP, (k + 1) * POOL_GROUP)
        src_ref, src_cols, width = pool_scr, cols, 1
        bufs = (tmp_a, tmp_b)
        stage = 0
        while 2 * width < win:
            start = SUBLANES * (stage + 1)
            dst = bufs[stage % 2]
            dst[start:ext, :] = (src_ref[start:ext, src_cols]
                                 + src_ref[start - width:ext - width, src_cols])
            src_ref, src_cols = dst, slice(0, POOL_GROUP)
            width *= 2
            stage += 1
        wsum = (src_ref[POOL_HALO:ext, src_cols]
                + src_ref[POOL_HALO - width:ext - width, src_cols])
        denom = jnp.minimum(row + 1, win).astype(_F32)
        a_scr[:, cols] = (wsum / denom - pool_scr[POOL_HALO:ext, cols]).astype(_BF16)
    pool_scr[0:POOL_HALO, :] = pool_scr[ts:ts + POOL_HALO, :]

    s0[...] = _dot(v_scr[...], w_b_out_ref[...])
    project(s3, OFF_G + 2 * D_MODEL)
    merged_scr[...] += _sigmoid(z(s2, OFF_G + D_MODEL)) * s0[...]

    for k in range(len(POOL_WINDOWS)):
        cols = slice(k * POOL_GROUP, (k + 1) * POOL_GROUP)
        s1[:, cols] = _dot(a_scr[:, cols], w_pool_ref[k])
    merged_scr[...] += _sigmoid(z(s3, OFF_G + 2 * D_MODEL)) * (s1[...] * pscale_ref[...])

    def merge_rows(rows):
        return _dot(merged_scr[rows, :].astype(_BF16), w_o_ref[...])

    _residual_post_norm(x_ref, o_ref, gate, ln_g_ref, ln_b_ref, merge_rows)


def _channel_mix_kernel(x_ref, ada_ref, w_up_ref, b_up_ref, conv_w_ref, conv_b_ref,
                        w_down_ref, ln_g_ref, ln_b_ref, o_ref,
                        up_scr, conv_scr, f_scr, acc_scr):
    ts = x_ref.shape[1]

    @pl.when(pl.program_id(1) == 0)
    def _():
        conv_scr[0:CONV_HALO, :] = jnp.broadcast_to(-b_up_ref[:, 0:D_FF], (CONV_HALO, D_FF))

    shift, scale, gate = _ada_rows(ada_ref, 3)
    h = (x_ref[0] * (1.0 + scale) + shift).astype(_BF16)

    def project(lo, n):
        conv_scr[CONV_HALO:CONV_HALO + ts, lo:lo + n] = _dot(h, w_up_ref[:, lo:lo + n])
        up_scr[:, lo:lo + n] = _dot(h, w_up_ref[:, D_FF + lo:D_FF + lo + n])

    def gated_block(lo, n):
        w = conv_w_ref[:, lo:lo + n]
        bias = conv_b_ref[:, lo:lo + n] + b_up_ref[:, lo:lo + n] * jnp.sum(w, axis=0, keepdims=True)
        act = _gelu_tanh(_causal_conv3(conv_scr, w, lo, n, ts) + bias)
        up_g = up_scr[:, lo:lo + n] + b_up_ref[:, D_FF + lo:D_FF + lo + n]
        f_scr[:, lo:lo + n] = (act * up_g).astype(_BF16)

    blocks = _col_blocks(0, D_FF)
    assert D_FF_SPLIT // COL_BLOCK <= len(blocks) - 2
    project(*blocks[0])
    for i in range(1, len(blocks)):
        project(*blocks[i])
        if i == len(blocks) - 1:
            acc_scr[...] = _dot(f_scr[:, 0:D_FF_SPLIT], w_down_ref[0:D_FF_SPLIT, :])
        gated_block(*blocks[i - 1])
    gated_block(*blocks[-1])
    conv_scr[0:CONV_HALO, :] = conv_scr[ts:ts + CONV_HALO, :]

    def down_rows(rows):
        return acc_scr[rows, :] + _dot(f_scr[rows, D_FF_SPLIT:D_FF],
                                       w_down_ref[D_FF_SPLIT:D_FF, :])

    _residual_post_norm(x_ref, o_ref, gate, ln_g_ref, ln_b_ref, down_rows)


def _ada_kernel(c_ref, w_ref, b_ref, o_ref):
    c = c_ref[...]
    c_act = (c * _sigmoid(c)).astype(_BF16)
    bias = b_ref[pl.ds(pl.program_id(0), 1), :]
    o_ref[0] = _dot(c_act, w_ref[0].astype(_BF16)) + bias


def _is_row_stack(array, layer):
    return layer is not None and array.ndim == 2


def _resident_spec(array, layer):
    if layer is None or _is_row_stack(array, layer):
        zeros = (0,) * array.ndim
        return pl.BlockSpec(array.shape, lambda b, s: zeros, pipeline_mode=pl.Buffered(1))
    zeros = (0,) * (array.ndim - 1)
    return pl.BlockSpec((None,) + array.shape[1:], lambda b, s: (layer,) + zeros,
                        pipeline_mode=pl.Buffered(1))


def _call_with_casts(body, grid, in_specs, out_spec, out_shape, operands, cast, name,
                     scratch_shapes=()):
    inner = grid[1]
    steps = grid[0] * inner
    n_in, n_cast = len(operands), len(cast)

    def cast_plan(w, src_layer):
        cols = w.shape[-1]
        rows = w.size // (w.shape[0] * cols)
        view = w.reshape(w.shape[0], rows, cols)
        stay = next(k for k in range(1, steps + 1)
                    if steps % k == 0 and (rows * k) % (steps * BF16_SUBLANES) == 0)
        slab = rows * stay // steps
        src = pl.BlockSpec((None, slab, cols),
                           lambda i, j: (src_layer, (i * inner + j) // stay, 0))
        dst = pl.BlockSpec((slab, cols), lambda i, j: ((i * inner + j) // stay, 0))
        return view, src, dst, jax.ShapeDtypeStruct((rows, cols), _BF16)

    plans = [cast_plan(w, l) for w, l in cast]

    def kernel_body(*refs):
        ins, cast_in = refs[:n_in], refs[n_in:n_in + n_cast]
        o_ref, cast_out = refs[n_in + n_cast], refs[n_in + n_cast + 1:n_in + 2 * n_cast + 1]
        for src, dst in zip(cast_in, cast_out):
            dst[...] = src[...].astype(_BF16)
        body(*ins, o_ref, *refs[n_in + 2 * n_cast + 1:])

    outs = pl.pallas_call(
        kernel_body,
        grid=grid,
        in_specs=list(in_specs) + [p[1] for p in plans],
        out_specs=[out_spec] + [p[2] for p in plans],
        out_shape=[out_shape] + [p[3] for p in plans],
        scratch_shapes=scratch_shapes,
        compiler_params=pltpu.CompilerParams(
            dimension_semantics=("arbitrary", "arbitrary"),
            vmem_limit_bytes=VMEM_LIMIT_BYTES),
        name=name,
    )(*operands, *[p[0] for p in plans])
    return outs[0], [o.reshape(w.shape[1:]) for o, (w, _) in zip(outs[1:], cast)]


def _seq_tiled_call(body, x, ada, layer, weights, scratch_shapes, name, cast=()):
    batch, seq, d = x.shape
    assert seq % SEQ_TILE == 0 and SEQ_TILE % GMLP_BLOCK == 0 and SEQ_TILE % ROW_BLOCK == 0
    x_spec = pl.BlockSpec((1, SEQ_TILE, d), lambda b, s: (b, s, 0))
    ada_spec = pl.BlockSpec((None,) + ada.shape[1:], lambda b, s: (layer, 0, 0),
                            pipeline_mode=pl.Buffered(1))
    row_stacks = [2 + i for i, (w, l) in enumerate(weights) if _is_row_stack(w, l)]

    def layer_body(*refs):
        refs = list(refs)
        for i in row_stacks:
            per_layer = refs[i].shape[0] // DEPTH
            refs[i] = refs[i].at[layer * per_layer:(layer + 1) * per_layer]
        body(*refs)

    return _call_with_casts(
        layer_body, (batch, seq // SEQ_TILE),
        [x_spec, ada_spec] + [_resident_spec(w, l) for w, l in weights],
        x_spec, jax.ShapeDtypeStruct(x.shape, x.dtype),
        [x, ada] + [w for w, _ in weights], cast, name, scratch_shapes)


def _token_mix(x, ada, layer, weights, cast):
    ts = SEQ_TILE
    scratch = [pltpu.VMEM((ts, D_MODEL), _F32) for _ in range(4)] + [
        pltpu.VMEM((CONV_HALO + ts, D_CONV), _F32),
        pltpu.VMEM((POOL_HALO + ts, D_POOL), _F32),
        pltpu.VMEM((POOL_HALO + ts, POOL_GROUP), _F32),
        pltpu.VMEM((POOL_HALO + ts, POOL_GROUP), _F32),
        pltpu.VMEM((ts, D_CONV), _BF16),
        pltpu.VMEM((ts, D_GMLP), _BF16),
        pltpu.VMEM((ts, D_MODEL), _F32),
    ]
    return _seq_tiled_call(_token_mix_kernel, x, ada, layer, weights, scratch, "token_mix", cast)


def _channel_mix(x, ada, layer, weights, cast):
    ts = SEQ_TILE
    scratch = [
        pltpu.VMEM((ts, D_FF), _F32),
        pltpu.VMEM((CONV_HALO + ts, D_FF), _F32),
        pltpu.VMEM((ts, D_FF), _BF16),
        pltpu.VMEM((ts, D_MODEL), _F32),
    ]
    return _seq_tiled_call(_channel_mix_kernel, x, ada, layer, weights, scratch, "channel_mix",
                           cast)


def _ada_modulation(c, w_ada, b_ada, cast):
    depth, d, n_out = w_ada.shape
    batch = c.shape[0]
    rows = -(-batch // SUBLANES) * SUBLANES
    c_pad = jnp.pad(c, ((0, rows - batch), (0, 0)))
    col_block = ADA_COL_BLOCK
    assert n_out % col_block == 0
    return _call_with_casts(
        _ada_kernel, (depth, n_out // col_block),
        [pl.BlockSpec((rows, d), lambda l, j: (0, 0)),
         pl.BlockSpec((1, d, col_block), lambda l, j: (l, 0, j)),
         pl.BlockSpec((depth, col_block), lambda l, j: (0, j))],
        pl.BlockSpec((1, rows, col_block), lambda l, j: (l, 0, j)),
        jax.ShapeDtypeStruct((depth, rows, n_out), _F32),
        [c_pad, w_ada, b_ada], cast, "ada_modulation")


def kernel(x, c, w_ada, b_ada, w_in, b_in, conv_a, w_a_out, ln_v_g, ln_v_b, w_spatial, b_spatial, w_b_out, w_pool, pool_scale, w_o, ln1_g, ln1_b, w_up, b_up, conv_ffn, conv_ffn_b, w_down, ln2_g, ln2_b):
    depth = w_in.shape[0]
    token_f32 = (w_in, w_a_out, w_spatial, w_b_out, w_pool, w_o)
    channel_f32 = (w_up, w_down)
    ada, token_h = _ada_modulation(c, w_ada, b_ada, [(w, 0) for w in token_f32])

    b_sp = jnp.repeat(jnp.swapaxes(b_spatial, 1, 2), GMLP_GROUP, axis=2)

    def token_weights(layer, w_in_h, w_a_out_h, w_sp_h, w_b_out_h, w_pool_h, w_o_h):
        return [(w_in_h, None), (b_in, layer), (conv_a, layer), (w_a_out_h, None),
                (ln_v_g, layer), (ln_v_b, layer), (w_sp_h, None), (b_sp, layer),
                (w_b_out_h, None), (w_pool_h, None), (pool_scale, layer), (w_o_h, None),
                (ln1_g, layer), (ln1_b, layer)]

    def channel_weights(layer, w_up_h, w_down_h):
        return [(w_up_h, None), (b_up, layer), (conv_ffn, layer), (conv_ffn_b, layer),
                (w_down_h, None), (ln2_g, layer), (ln2_b, layer)]

    for layer in range(depth):
        x, channel_h = _token_mix(x, ada, layer, token_weights(layer, *token_h),
                                  [(w, layer) for w in channel_f32])
        nxt = [(w, layer + 1) for w in token_f32] if layer + 1 < depth else []
        x, token_h = _channel_mix(x, ada, layer, channel_weights(layer, *channel_h), nxt)
    return x
```

```python
import jax
import jax.numpy as jnp
from jax.experimental import pallas as pl
from jax.experimental.pallas import tpu as pltpu

D_MODEL = 1024
DEPTH = 2
CHUNK = 64
CONV_WIDTH = 3
D_CONV = 1024
D_GMLP = 1024
GMLP_BLOCK = 128
N_GROUPS_GMLP = 8
GMLP_GROUP = D_GMLP // N_GROUPS_GMLP
D_POOL = 1024
POOL_WINDOWS = (2, 4, 8, 16)
POOL_GROUP = D_POOL // len(POOL_WINDOWS)
D_FF = 2816
D_IN = 3 * D_CONV + 2 * D_GMLP + D_POOL + 3 * D_MODEL
ALPHA = (2 * DEPTH) ** 0.25
LN_EPS = 1e-5
GELU_C = 0.7978845608028654
GELU_A = 0.044715

OFF_B, OFF_C, OFF_X = 0, D_CONV, 2 * D_CONV
OFF_U = 3 * D_CONV
OFF_V = OFF_U + D_GMLP
OFF_P = OFF_V + D_GMLP
OFF_G = OFF_P + D_POOL

SUBLANES = 8
V7X_MXU_DIM = 256
SEQ_TILE = 512
ROW_BLOCK = 256
CONV_HALO = SUBLANES
POOL_HALO = 32
COL_BLOCK = 512
ADA_COL_BLOCK = 1536
D_FF_SPLIT = 4 * COL_BLOCK
assert D_FF_SPLIT % V7X_MXU_DIM == 0 and (D_FF - D_FF_SPLIT) % V7X_MXU_DIM == 0
LANES = 128
BF16_SUBLANES = 2 * SUBLANES
V7X_VMEM_BYTES = 64 * 1024 * 1024
VMEM_LIMIT_BYTES = V7X_VMEM_BYTES * 15 // 16

_F32 = jnp.float32
_BF16 = jnp.bfloat16


def _gelu_tanh(x):
    inner = x * (GELU_C + (GELU_C * GELU_A) * (x * x))
    half = 0.5 * x
    return half + half * jnp.tanh(inner)


def _sigmoid(x):
    return 0.5 * (jnp.tanh(0.5 * x) + 1.0)


def _layer_norm(r, g, b):
    mu = jnp.mean(r, axis=-1, keepdims=True)
    d = r - mu
    var = jnp.mean(d * d, axis=-1, keepdims=True)
    return d * jax.lax.rsqrt(var + LN_EPS) * g + b


def _runtime_zero(x):
    bits = pltpu.bitcast(x, jnp.uint32)
    return pltpu.bitcast((bits >> 16) >> 16, x.dtype)


def _dot(a, b):
    return jnp.dot(a, b, preferred_element_type=_F32)


def _col_blocks(lo, hi):
    return [(c, min(COL_BLOCK, hi - c)) for c in range(lo, hi, COL_BLOCK)]


def _causal_conv3(scr, w, lo, n, ts):
    base = CONV_HALO - (CONV_WIDTH - 1)
    acc = None
    for k in range(CONV_WIDTH):
        term = w[k:k + 1, :] * scr[base + k:base + k + ts, lo:lo + n]
        acc = term if acc is None else acc + term
    return acc


def _ada_rows(ada_ref, first):
    batch_row = pl.ds(pl.program_id(0), 1)
    return [ada_ref[batch_row, k * D_MODEL:(k + 1) * D_MODEL] for k in range(first, first + 3)]


def _residual_post_norm(x_ref, o_ref, gate, ln_g_ref, ln_b_ref, sublayer_rows):
    ts = x_ref.shape[1]
    for r in range(0, ts, ROW_BLOCK):
        rows = slice(r, r + ROW_BLOCK)
        res = ALPHA * x_ref[0, rows, :] + gate * sublayer_rows(rows)
        o_ref[0, rows, :] = _layer_norm(res, ln_g_ref[...], ln_b_ref[...])


def _token_mix_kernel(x_ref, ada_ref, w_in_ref, b_in_ref, conv_a_ref, w_a_out_ref,
                      lnv_g_ref, lnv_b_ref, w_sp_ref, b_sp_ref, w_b_out_ref,
                      w_pool_ref, pscale_ref, w_o_ref, ln_g_ref, ln_b_ref,
                      o_ref,
                      s0, s1, s2, s3, conv_scr, pool_scr, tmp_a, tmp_b, a_scr, v_scr,
                      merged_scr):
    ts = x_ref.shape[1]
    seq_tile = pl.program_id(1)

    @pl.when(seq_tile == 0)
    def _():
        conv_scr[0:CONV_HALO, :] = jnp.zeros((CONV_HALO, D_CONV), _F32)
        pool_scr[0:POOL_HALO, :] = jnp.zeros((POOL_HALO, D_POOL), _F32)

    shift, scale, gate = _ada_rows(ada_ref, 0)
    h = (x_ref[0] * (1.0 + scale) + shift).astype(_BF16)

    def project(slot, off):
        for c, n in _col_blocks(0, D_MODEL):
            slot[:, c:c + n] = _dot(h, w_in_ref[:, off + c:off + c + n])

    def z(slot, off, lo=0, n=D_MODEL):
        return slot[:, lo:lo + n] + b_in_ref[:, off + lo:off + lo + n]

    project(s0, OFF_C)
    project(s1, OFF_X)
    project(s2, OFF_B)
    conv_scr[CONV_HALO:CONV_HALO + ts, :] = z(s0, OFF_C) * z(s1, OFF_X)

    project(s3, OFF_V)
    for lo, n in _col_blocks(0, D_CONV):
        cv = _causal_conv3(conv_scr, conv_a_ref[:, lo:lo + n], lo, n, ts)
        a_scr[:, lo:lo + n] = (z(s2, OFF_B, lo, n) * cv).astype(_BF16)
    conv_scr[0:CONV_HALO, :] = conv_scr[ts:ts + CONV_HALO, :]

    project(s0, OFF_U)
    project(s1, OFF_G)
    quarter = ts // 4
    anchors = (None, s0[0:SUBLANES, D_MODEL - LANES:D_MODEL],
               s1[0:SUBLANES, COL_BLOCK - LANES:COL_BLOCK],
               s1[0:SUBLANES, D_MODEL - LANES:D_MODEL])
    for q, anchor in enumerate(anchors):
        rows = slice(q * quarter, (q + 1) * quarter)
        y = _gelu_tanh(s3[rows, :] + b_in_ref[:, OFF_V:OFF_V + D_GMLP])
        mu = jnp.mean(y, axis=-1, keepdims=True)
        if anchor is not None:
            mu = mu + jnp.tile(_runtime_zero(anchor)[:, 0:1], (quarter // SUBLANES, 1))
        d = y - mu
        var = jnp.mean(d * d, axis=-1, keepdims=True)
        v = d * jax.lax.rsqrt(var + LN_EPS) * lnv_g_ref[...] + lnv_b_ref[...]
        v_scr[rows, :] = v.astype(_BF16)

    s2[...] = _dot(a_scr[...], w_a_out_ref[...])

    chunk_shift = CHUNK.bit_length() - 1
    assert CHUNK == 1 << chunk_shift
    block_idx = (GMLP_BLOCK, GMLP_BLOCK)
    pos = jax.lax.broadcasted_iota(jnp.int32, block_idx, 0) >> chunk_shift
    src = jax.lax.broadcasted_iota(jnp.int32, block_idx, 1) >> chunk_shift
    allowed = src <= pos
    for g in range(N_GROUPS_GMLP):
        cols = slice(g * GMLP_GROUP, (g + 1) * GMLP_GROUP)
        w_g = jnp.where(allowed, w_sp_ref[g], jnp.zeros((), _BF16))
        n_blocks = ts // GMLP_BLOCK
        rhs = jnp.concatenate([v_scr[n * GMLP_BLOCK:(n + 1) * GMLP_BLOCK, cols]
                               for n in range(n_blocks)], axis=1)
        mixed = _dot(w_g, rhs)
        for n in range(n_blocks):
            s3[n * GMLP_BLOCK:(n + 1) * GMLP_BLOCK, cols] = (
                mixed[:, n * GMLP_BLOCK:(n + 1) * GMLP_BLOCK])

    merged_scr[...] = _sigmoid(z(s1, OFF_G)) * s2[...]
    project(s1, OFF_P)
    for n in range(ts // GMLP_BLOCK):
        rows = slice(n * GMLP_BLOCK, (n + 1) * GMLP_BLOCK)
        u = _gelu_tanh(s0[rows, :] + b_in_ref[:, OFF_U:OFF_U + D_GMLP])
        v_scr[rows, :] = (u * (s3[rows, :] + b_sp_ref[...])).astype(_BF16)

    project(s2, OFF_G + D_MODEL)
    pool_scr[POOL_HALO:POOL_HALO + ts, :] = z(s1, OFF_P)
    row = seq_tile * ts + jax.lax.broadcasted_iota(jnp.int32, (ts, 1), 0)
    ext = POOL_HALO + ts
    for k, win in enumerate(POOL_WINDOWS):
        cols = slice(k * POOL_GROUP, (k + 1) * POOL_GROUP)
        src_ref, src_cols, width = pool_scr, cols, 1
        bufs = (tmp_a, tmp_b)
        stage = 0
        while 2 * width < win:
            start = SUBLANES * (stage + 1)
            dst = bufs[stage % 2]
            dst[start:ext, :] = (src_ref[start:ext, src_cols]
                                 + src_ref[start - width:ext - width, src_cols])
            src_ref, src_cols = dst, slice(0, POOL_GROUP)
            width *= 2
            stage += 1
        wsum = (src_ref[POOL_HALO:ext, src_cols]
                + src_ref[POOL_HALO - width:ext - width, src_cols])
        denom = jnp.minimum(row + 1, win).astype(_F32)
        a_scr[:, cols] = (wsum / denom - pool_scr[POOL_HALO:ext, cols]).astype(_BF16)
    pool_scr[0:POOL_HALO, :] = pool_scr[ts:ts + POOL_HALO, :]

    s0[...] = _dot(v_scr[...], w_b_out_ref[...])
    project(s3, OFF_G + 2 * D_MODEL)
    merged_scr[...] += _sigmoid(z(s2, OFF_G + D_MODEL)) * s0[...]

    for k in range(len(POOL_WINDOWS)):
        cols = slice(k * POOL_GROUP, (k + 1) * POOL_GROUP)
        s1[:, cols] = _dot(a_scr[:, cols], w_pool_ref[k])
    merged_scr[...] += _sigmoid(z(s3, OFF_G + 2 * D_MODEL)) * (s1[...] * pscale_ref[...])

    def merge_rows(rows):
        return _dot(merged_scr[rows, :].astype(_BF16), w_o_ref[...])

    _residual_post_norm(x_ref, o_ref, gate, ln_g_ref, ln_b_ref, merge_rows)


def _channel_mix_kernel(x_ref, ada_ref, w_up_ref, b_up_ref, conv_w_ref, conv_b_ref,
                        w_down_ref, ln_g_ref, ln_b_ref, o_ref,
                        up_scr, conv_scr, f_scr, acc_scr):
    ts = x_ref.shape[1]

    @pl.when(pl.program_id(1) == 0)
    def _():
        conv_scr[0:CONV_HALO, :] = jnp.broadcast_to(-b_up_ref[:, 0:D_FF], (CONV_HALO, D_FF))

    shift, scale, gate = _ada_rows(ada_ref, 3)
    h = (x_ref[0] * (1.0 + scale) + shift).astype(_BF16)

    def project(lo, n):
        conv_scr[CONV_HALO:CONV_HALO + ts, lo:lo + n] = _dot(h, w_up_ref[:, lo:lo + n])
        up_scr[:, lo:lo + n] = _dot(h, w_up_ref[:, D_FF + lo:D_FF + lo + n])

    def gated_block(lo, n):
        w = conv_w_ref[:, lo:lo + n]
        bias = conv_b_ref[:, lo:lo + n] + b_up_ref[:, lo:lo + n] * jnp.sum(w, axis=0, keepdims=True)
        act = _gelu_tanh(_causal_conv3(conv_scr, w, lo, n, ts) + bias)
        up_g = up_scr[:, lo:lo + n] + b_up_ref[:, D_FF + lo:D_FF + lo + n]
        f_scr[:, lo:lo + n] = (act * up_g).astype(_BF16)

    blocks = _col_blocks(0, D_FF)
    assert D_FF_SPLIT // COL_BLOCK <= len(blocks) - 2
    project(*blocks[0])
    for i in range(1, len(blocks)):
        project(*blocks[i])
        if i == len(blocks) - 1:
            acc_scr[...] = _dot(f_scr[:, 0:D_FF_SPLIT], w_down_ref[0:D_FF_SPLIT, :])
        gated_block(*blocks[i - 1])
    gated_block(*blocks[-1])
    conv_scr[0:CONV_HALO, :] = conv_scr[ts:ts + CONV_HALO, :]

    def down_rows(rows):
        return acc_scr[rows, :] + _dot(f_scr[rows, D_FF_SPLIT:D_FF],
                                       w_down_ref[D_FF_SPLIT:D_FF, :])

    _residual_post_norm(x_ref, o_ref, gate, ln_g_ref, ln_b_ref, down_rows)


def _ada_kernel(c_ref, w_ref, b_ref, o_ref):
    c = c_ref[...]
    c_act = (c * _sigmoid(c)).astype(_BF16)
    bias = b_ref[pl.ds(pl.program_id(0), 1), :]
    o_ref[0] = _dot(c_act, w_ref[0].astype(_BF16)) + bias


def _is_row_stack(array, layer):
    return layer is not None and array.ndim == 2


def _resident_spec(array, layer):
    if layer is None or _is_row_stack(array, layer):
        zeros = (0,) * array.ndim
        return pl.BlockSpec(array.shape, lambda b, s: zeros, pipeline_mode=pl.Buffered(1))
    zeros = (0,) * (array.ndim - 1)
    return pl.BlockSpec((None,) + array.shape[1:], lambda b, s: (layer,) + zeros,
                        pipeline_mode=pl.Buffered(1))


def _call_with_casts(body, grid, in_specs, out_spec, out_shape, operands, cast, name,
                     scratch_shapes=()):
    inner = grid[1]
    steps = grid[0] * inner
    n_in, n_cast = len(operands), len(cast)

    def cast_plan(w, src_layer):
        cols = w.shape[-1]
        rows = w.size // (w.shape[0] * cols)
        view = w.reshape(w.shape[0], rows, cols)
        stay = next(k for k in range(1, steps + 1)
                    if steps % k == 0 and (rows * k) % (steps * BF16_SUBLANES) == 0)
        slab = rows * stay // steps
        src = pl.BlockSpec((None, slab, cols),
                           lambda i, j: (src_layer, (i * inner + j) // stay, 0))
        dst = pl.BlockSpec((slab, cols), lambda i, j: ((i * inner + j) // stay, 0))
        return view, src, dst, jax.ShapeDtypeStruct((rows, cols), _BF16)

    plans = [cast_plan(w, l) for w, l in cast]

    def kernel_body(*refs):
        ins, cast_in = refs[:n_in], refs[n_in:n_in + n_cast]
        o_ref, cast_out = refs[n_in + n_cast], refs[n_in + n_cast + 1:n_in + 2 * n_cast + 1]
        for src, dst in zip(cast_in, cast_out):
            dst[...] = src[...].astype(_BF16)
        body(*ins, o_ref, *refs[n_in + 2 * n_cast + 1:])

    outs = pl.pallas_call(
        kernel_body,
        grid=grid,
        in_specs=list(in_specs) + [p[1] for p in plans],
        out_specs=[out_spec] + [p[2] for p in plans],
        out_shape=[out_shape] + [p[3] for p in plans],
        scratch_shapes=scratch_shapes,
        compiler_params=pltpu.CompilerParams(
            dimension_semantics=("arbitrary", "arbitrary"),
            vmem_limit_bytes=VMEM_LIMIT_BYTES),
        name=name,
    )(*operands, *[p[0] for p in plans])
    return outs[0], [o.reshape(w.shape[1:]) for o, (w, _) in zip(outs[1:], cast)]


def _seq_tiled_call(body, x, ada, layer, weights, scratch_shapes, name, cast=()):
    batch, seq, d = x.shape
    assert seq % SEQ_TILE == 0 and SEQ_TILE % GMLP_BLOCK == 0 and SEQ_TILE % ROW_BLOCK == 0
    x_spec = pl.BlockSpec((1, SEQ_TILE, d), lambda b, s: (b, s, 0))
    ada_spec = pl.BlockSpec((None,) + ada.shape[1:], lambda b, s: (layer, 0, 0),
                            pipeline_mode=pl.Buffered(1))
    row_stacks = [2 + i for i, (w, l) in enumerate(weights) if _is_row_stack(w, l)]

    def layer_body(*refs):
        refs = list(refs)
        for i in row_stacks:
            per_layer = refs[i].shape[0] // DEPTH
            refs[i] = refs[i].at[layer * per_layer:(layer + 1) * per_layer]
        body(*refs)

    return _call_with_casts(
        layer_body, (batch, seq // SEQ_TILE),
        [x_spec, ada_spec] + [_resident_spec(w, l) for w, l in weights],
        x_spec, jax.ShapeDtypeStruct(x.shape, x.dtype),
        [x, ada] + [w for w, _ in weights], cast, name, scratch_shapes)


def _token_mix(x, ada, layer, weights, cast):
    ts = SEQ_TILE
    scratch = [pltpu.VMEM((ts, D_MODEL), _F32) for _ in range(4)] + [
        pltpu.VMEM((CONV_HALO + ts, D_CONV), _F32),
        pltpu.VMEM((POOL_HALO + ts, D_POOL), _F32),
        pltpu.VMEM((POOL_HALO + ts, POOL_GROUP), _F32),
        pltpu.VMEM((POOL_HALO + ts, POOL_GROUP), _F32),
        pltpu.VMEM((ts, D_CONV), _BF16),
        pltpu.VMEM((ts, D_GMLP), _BF16),
        pltpu.VMEM((ts, D_MODEL), _F32),
    ]
    return _seq_tiled_call(_token_mix_kernel, x, ada, layer, weights, scratch, "token_mix", cast)


def _channel_mix(x, ada, layer, weights, cast):
    ts = SEQ_TILE
    scratch = [
        pltpu.VMEM((ts, D_FF), _F32),
        pltpu.VMEM((CONV_HALO + ts, D_FF), _F32),
        pltpu.VMEM((ts, D_FF), _BF16),
        pltpu.VMEM((ts, D_MODEL), _F32),
    ]
    return _seq_tiled_call(_channel_mix_kernel, x, ada, layer, weights, scratch, "channel_mix",
                           cast)


def _ada_modulation(c, w_ada, b_ada, cast):
    depth, d, n_out = w_ada.shape
    batch = c.shape[0]
    rows = -(-batch // SUBLANES) * SUBLANES
    c_pad = jnp.pad(c, ((0, rows - batch), (0, 0)))
    col_block = ADA_COL_BLOCK
    assert n_out % col_block == 0
    return _call_with_casts(
        _ada_kernel, (depth, n_out // col_block),
        [pl.BlockSpec((rows, d), lambda l, j: (0, 0)),
         pl.BlockSpec((1, d, col_block), lambda l, j: (l, 0, j)),
         pl.BlockSpec((depth, col_block), lambda l, j: (0, j))],
        pl.BlockSpec((1, rows, col_block), lambda l, j: (l, 0, j)),
        jax.ShapeDtypeStruct((depth, rows, n_out), _F32),
        [c_pad, w_ada, b_ada], cast, "ada_modulation")


def kernel(x, c, w_ada, b_ada, w_in, b_in, conv_a, w_a_out, ln_v_g, ln_v_b, w_spatial, b_spatial, w_b_out, w_pool, pool_scale, w_o, ln1_g, ln1_b, w_up, b_up, conv_ffn, conv_ffn_b, w_down, ln2_g, ln2_b):
    depth = w_in.shape[0]
    token_f32 = (w_in, w_a_out, w_spatial, w_b_out, w_pool, w_o)
    channel_f32 = (w_up, w_down)
    ada, token_h = _ada_modulation(c, w_ada, b_ada, [(w, 0) for w in token_f32])

    b_sp = jnp.repeat(jnp.swapaxes(b_spatial, 1, 2), GMLP_GROUP, axis=2)

    def token_weights(layer, w_in_h, w_a_out_h, w_sp_h, w_b_out_h, w_pool_h, w_o_h):
        return [(w_in_h, None), (b_in, layer), (conv_a, layer), (w_a_out_h, None),
                (ln_v_g, layer), (ln_v_b, layer), (w_sp_h, None), (b_sp, layer),
                (w_b_out_h, None), (w_pool_h, None), (pool_scale, layer), (w_o_h, None),
                (ln1_g, layer), (ln1_b, layer)]

    def channel_weights(layer, w_up_h, w_down_h):
        return [(w_up_h, None), (b_up, layer), (conv_ffn, layer), (conv_ffn_b, layer),
                (w_down_h, None), (ln2_g, layer), (ln2_b, layer)]

    for layer in range(depth):
        x, channel_h = _token_mix(x, ada, layer, token_weights(layer, *token_h),
                                  [(w, layer) for w in channel_f32])
        nxt = [(w, layer + 1) for w in token_f32] if layer + 1 < depth else []
        x, token_h = _channel_mix(x, ada, layer, channel_weights(layer, *channel_h), nxt)
    return x
```

```python
import jax
import jax.numpy as jnp
from jax.experimental import pallas as pl
from jax.experimental.pallas import tpu as pltpu

D_MODEL = 1024
DEPTH = 2
CHUNK = 64
CONV_WIDTH = 3
D_CONV = 1024
D_GMLP = 1024
GMLP_BLOCK = 128
N_GROUPS_GMLP = 8
GMLP_GROUP = D_GMLP // N_GROUPS_GMLP
D_POOL = 1024
POOL_WINDOWS = (2, 4, 8, 16)
POOL_GROUP = D_POOL // len(POOL_WINDOWS)
D_FF = 2816
D_IN = 3 * D_CONV + 2 * D_GMLP + D_POOL + 3 * D_MODEL
ALPHA = (2 * DEPTH) ** 0.25
LN_EPS = 1e-5
GELU_C = 0.7978845608028654
GELU_A = 0.044715

OFF_B, OFF_C, OFF_X = 0, D_CONV, 2 * D_CONV
OFF_U = 3 * D_CONV
OFF_V = OFF_U + D_GMLP
OFF_P = OFF_V + D_GMLP
OFF_G = OFF_P + D_POOL

SUBLANES = 8
V7X_MXU_DIM = 256
SEQ_TILE = 512
ROW_BLOCK = 256
CONV_HALO = SUBLANES
POOL_HALO = 32
COL_BLOCK = 512
ADA_COL_BLOCK = 1536
D_FF_SPLIT = 4 * COL_BLOCK
assert D_FF_SPLIT % V7X_MXU_DIM == 0 and (D_FF - D_FF_SPLIT) % V7X_MXU_DIM == 0
LANES = 128
BF16_SUBLANES = 2 * SUBLANES
V7X_VMEM_BYTES = 64 * 1024 * 1024
VMEM_LIMIT_BYTES = V7X_VMEM_BYTES * 15 // 16

_F32 = jnp.float32
_BF16 = jnp.bfloat16


def _gelu_tanh(x):
    inner = x * (GELU_C + (GELU_C * GELU_A) * (x * x))
    half = 0.5 * x
    return half + half * jnp.tanh(inner)


def _sigmoid(x):
    return 0.5 * (jnp.tanh(0.5 * x) + 1.0)


def _layer_norm(r, g, b):
    mu = jnp.mean(r, axis=-1, keepdims=True)
    d = r - mu
    var = jnp.mean(d * d, axis=-1, keepdims=True)
    return d * jax.lax.rsqrt(var + LN_EPS) * g + b


def _runtime_zero(x):
    bits = pltpu.bitcast(x, jnp.uint32)
    return pltpu.bitcast((bits >> 16) >> 16, x.dtype)


def _dot(a, b):
    return jnp.dot(a, b, preferred_element_type=_F32)


def _col_blocks(lo, hi):
    return [(c, min(COL_BLOCK, hi - c)) for c in range(lo, hi, COL_BLOCK)]


def _causal_conv3(scr, w, lo, n, ts):
    base = CONV_HALO - (CONV_WIDTH - 1)
    acc = None
    for k in range(CONV_WIDTH):
        term = w[k:k + 1, :] * scr[base + k:base + k + ts, lo:lo + n]
        acc = term if acc is None else acc + term
    return acc


def _ada_rows(ada_ref, first):
    batch_row = pl.ds(pl.program_id(0), 1)
    return [ada_ref[batch_row, k * D_MODEL:(k + 1) * D_MODEL] for k in range(first, first + 3)]


def _residual_post_norm(x_ref, o_ref, gate, ln_g_ref, ln_b_ref, sublayer_rows):
    ts = x_ref.shape[1]
    for r in range(0, ts, ROW_BLOCK):
        rows = slice(r, r + ROW_BLOCK)
        res = ALPHA * x_ref[0, rows, :] + gate * sublayer_rows(rows)
        o_ref[0, rows, :] = _layer_norm(res, ln_g_ref[...], ln_b_ref[...])


def _token_mix_kernel(x_ref, ada_ref, w_in_ref, b_in_ref, conv_a_ref, w_a_out_ref,
                      lnv_g_ref, lnv_b_ref, w_sp_ref, b_sp_ref, w_b_out_ref,
                      w_pool_ref, pscale_ref, w_o_ref, ln_g_ref, ln_b_ref,
                      o_ref,
                      s0, s1, s2, s3, conv_scr, pool_scr, tmp_a, tmp_b, a_scr, v_scr,
                      merged_scr):
    ts = x_ref.shape[1]
    seq_tile = pl.program_id(1)

    @pl.when(seq_tile == 0)
    def _():
        conv_scr[0:CONV_HALO, :] = jnp.zeros((CONV_HALO, D_CONV), _F32)
        pool_scr[0:POOL_HALO, :] = jnp.zeros((POOL_HALO, D_POOL), _F32)

    shift, scale, gate = _ada_rows(ada_ref, 0)
    h = (x_ref[0] * (1.0 + scale) + shift).astype(_BF16)

    def project(slot, off):
        for c, n in _col_blocks(0, D_MODEL):
            slot[:, c:c + n] = _dot(h, w_in_ref[:, off + c:off + c + n])

    def z(slot, off, lo=0, n=D_MODEL):
        return slot[:, lo:lo + n] + b_in_ref[:, off + lo:off + lo + n]

    project(s0, OFF_C)
    project(s1, OFF_X)
    project(s2, OFF_B)
    conv_scr[CONV_HALO:CONV_HALO + ts, :] = z(s0, OFF_C) * z(s1, OFF_X)

    project(s3, OFF_V)
    for lo, n in _col_blocks(0, D_CONV):
        cv = _causal_conv3(conv_scr, conv_a_ref[:, lo:lo + n], lo, n, ts)
        a_scr[:, lo:lo + n] = (z(s2, OFF_B, lo, n) * cv).astype(_BF16)
    conv_scr[0:CONV_HALO, :] = conv_scr[ts:ts + CONV_HALO, :]

    project(s0, OFF_U)
    project(s1, OFF_G)
    quarter = ts // 4
    anchors = (None, s0[0:SUBLANES, D_MODEL - LANES:D_MODEL],
               s1[0:SUBLANES, COL_BLOCK - LANES:COL_BLOCK],
               s1[0:SUBLANES, D_MODEL - LANES:D_MODEL])
    for q, anchor in enumerate(anchors):
        rows = slice(q * quarter, (q + 1) * quarter)
        bias = b_in_ref[:, OFF_V:OFF_V + D_GMLP]
        if anchor is not None and q >= 2:
            bias = bias + jnp.tile(_runtime_zero(anchor)[0:1, :], (1, D_GMLP // LANES))
        y = _gelu_tanh(s3[rows, :] + bias)
        mu = jnp.mean(y, axis=-1, keepdims=True)
        if anchor is not None and q < 2:
            mu = mu + jnp.tile(_runtime_zero(anchor)[:, 0:1], (quarter // SUBLANES, 1))
        d = y - mu
        var = jnp.mean(d * d, axis=-1, keepdims=True)
        v = d * jax.lax.rsqrt(var + LN_EPS) * lnv_g_ref[...] + lnv_b_ref[...]
        v_scr[rows, :] = v.astype(_BF16)

    s2[...] = _dot(a_scr[...], w_a_out_ref[...])

    chunk_shift = CHUNK.bit_length() - 1
    assert CHUNK == 1 << chunk_shift
    block_idx = (GMLP_BLOCK, GMLP_BLOCK)
    pos = jax.lax.broadcasted_iota(jnp.int32, block_idx, 0) >> chunk_shift
    src = jax.lax.broadcasted_iota(jnp.int32, block_idx, 1) >> chunk_shift
    allowed = src <= pos
    for g in range(N_GROUPS_GMLP):
        cols = slice(g * GMLP_GROUP, (g + 1) * GMLP_GROUP)
        w_g = jnp.where(allowed, w_sp_ref[g], jnp.zeros((), _BF16))
        n_blocks = ts // GMLP_BLOCK
        rhs = jnp.concatenate([v_scr[n * GMLP_BLOCK:(n + 1) * GMLP_BLOCK, cols]
                               for n in range(n_blocks)], axis=1)
        mixed = _dot(w_g, rhs)
        for n in range(n_blocks):
            s3[n * GMLP_BLOCK:(n + 1) * GMLP_BLOCK, cols] = (
                mixed[:, n * GMLP_BLOCK:(n + 1) * GMLP_BLOCK])

    merged_scr[...] = _sigmoid(z(s1, OFF_G)) * s2[...]
    project(s1, OFF_P)
    for n in range(ts // GMLP_BLOCK):
        rows = slice(n * GMLP_BLOCK, (n + 1) * GMLP_BLOCK)
        u = _gelu_tanh(s0[rows, :] + b_in_ref[:, OFF_U:OFF_U + D_GMLP])
        v_scr[rows, :] = (u * (s3[rows, :] + b_sp_ref[...])).astype(_BF16)

    project(s2, OFF_G + D_MODEL)
    pool_scr[POOL_HALO:POOL_HALO + ts, :] = z(s1, OFF_P)
    row = seq_tile * ts + jax.lax.broadcasted_iota(jnp.int32, (ts, 1), 0)
    ext = POOL_HALO + ts
    for k, win in enumerate(POOL_WINDOWS):
        cols = slice(k * POOL_GROUP, (k + 1) * POOL_GROUP)
        src_ref, src_cols, width = pool_scr, cols, 1
        bufs = (tmp_a, tmp_b)
        stage = 0
        while 2 * width < win:
            start = SUBLANES * (stage + 1)
            dst = bufs[stage % 2]
            dst[start:ext, :] = (src_ref[start:ext, src_cols]
                                 + src_ref[start - width:ext - width, src_cols])
            src_ref, src_cols = dst, slice(0, POOL_GROUP)
            width *= 2
            stage += 1
        wsum = (src_ref[POOL_HALO:ext, src_cols]
                + src_ref[POOL_HALO - width:ext - width, src_cols])
        denom = jnp.minimum(row + 1, win).astype(_F32)
        a_scr[:, cols] = (wsum / denom - pool_scr[POOL_HALO:ext, cols]).astype(_BF16)
    pool_scr[0:POOL_HALO, :] = pool_scr[ts:ts + POOL_HALO, :]

    s0[...] = _dot(v_scr[...], w_b_out_ref[...])
    project(s3, OFF_G + 2 * D_MODEL)
    merged_scr[...] += _sigmoid(z(s2, OFF_G + D_MODEL)) * s0[...]

    for k in range(len(POOL_WINDOWS)):
        cols = slice(k * POOL_GROUP, (k + 1) * POOL_GROUP)
        s1[:, cols] = _dot(a_scr[:, cols], w_pool_ref[k])
    merged_scr[...] += _sigmoid(z(s3, OFF_G + 2 * D_MODEL)) * (s1[...] * pscale_ref[...])

    def merge_rows(rows):
        return _dot(merged_scr[rows, :].astype(_BF16), w_o_ref[...])

    _residual_post_norm(x_ref, o_ref, gate, ln_g_ref, ln_b_ref, merge_rows)


def _channel_mix_kernel(x_ref, ada_ref, w_up_ref, b_up_ref, conv_w_ref, conv_b_ref,
                        w_down_ref, ln_g_ref, ln_b_ref, o_ref,
                        up_scr, conv_scr, f_scr, acc_scr):
    ts = x_ref.shape[1]

    @pl.when(pl.program_id(1) == 0)
    def _():
        conv_scr[0:CONV_HALO, :] = jnp.broadcast_to(-b_up_ref[:, 0:D_FF], (CONV_HALO, D_FF))

    shift, scale, gate = _ada_rows(ada_ref, 3)
    h = (x_ref[0] * (1.0 + scale) + shift).astype(_BF16)

    def project(lo, n):
        conv_scr[CONV_HALO:CONV_HALO + ts, lo:lo + n] = _dot(h, w_up_ref[:, lo:lo + n])
        up_scr[:, lo:lo + n] = _dot(h, w_up_ref[:, D_FF + lo:D_FF + lo + n])

    def gated_block(lo, n):
        w = conv_w_ref[:, lo:lo + n]
        bias = conv_b_ref[:, lo:lo + n] + b_up_ref[:, lo:lo + n] * jnp.sum(w, axis=0, keepdims=True)
        act = _gelu_tanh(_causal_conv3(conv_scr, w, lo, n, ts) + bias)
        up_g = up_scr[:, lo:lo + n] + b_up_ref[:, D_FF + lo:D_FF + lo + n]
        f_scr[:, lo:lo + n] = (act * up_g).astype(_BF16)

    blocks = _col_blocks(0, D_FF)
    assert D_FF_SPLIT // COL_BLOCK <= len(blocks) - 2
    project(*blocks[0])
    for i in range(1, len(blocks)):
        project(*blocks[i])
        if i == len(blocks) - 1:
            acc_scr[...] = _dot(f_scr[:, 0:D_FF_SPLIT], w_down_ref[0:D_FF_SPLIT, :])
        gated_block(*blocks[i - 1])
    gated_block(*blocks[-1])
    conv_scr[0:CONV_HALO, :] = conv_scr[ts:ts + CONV_HALO, :]

    def down_rows(rows):
        return acc_scr[rows, :] + _dot(f_scr[rows, D_FF_SPLIT:D_FF],
                                       w_down_ref[D_FF_SPLIT:D_FF, :])

    _residual_post_norm(x_ref, o_ref, gate, ln_g_ref, ln_b_ref, down_rows)


def _ada_kernel(c_ref, w_ref, b_ref, o_ref):
    c = c_ref[...]
    c_act = (c * _sigmoid(c)).astype(_BF16)
    bias = b_ref[pl.ds(pl.program_id(0), 1), :]
    o_ref[0] = _dot(c_act, w_ref[0].astype(_BF16)) + bias


def _is_row_stack(array, layer):
    return layer is not None and array.ndim == 2


def _resident_spec(array, layer):
    if layer is None or _is_row_stack(array, layer):
        zeros = (0,) * array.ndim
        return pl.BlockSpec(array.shape, lambda b, s: zeros, pipeline_mode=pl.Buffered(1))
    zeros = (0,) * (array.ndim - 1)
    return pl.BlockSpec((None,) + array.shape[1:], lambda b, s: (layer,) + zeros,
                        pipeline_mode=pl.Buffered(1))


def _call_with_casts(body, grid, in_specs, out_spec, out_shape, operands, cast, name,
                     scratch_shapes=()):
    inner = grid[1]
    steps = grid[0] * inner
    n_in, n_cast = len(operands), len(cast)

    def cast_plan(w, src_layer):
        cols = w.shape[-1]
        rows = w.size // (w.shape[0] * cols)
        view = w.reshape(w.shape[0], rows, cols)
        stay = next(k for k in range(1, steps + 1)
                    if steps % k == 0 and (rows * k) % (steps * BF16_SUBLANES) == 0)
        slab = rows * stay // steps
        src = pl.BlockSpec((None, slab, cols),
                           lambda i, j: (src_layer, (i * inner + j) // stay, 0))
        dst = pl.BlockSpec((slab, cols), lambda i, j: ((i * inner + j) // stay, 0))
        return view, src, dst, jax.ShapeDtypeStruct((rows, cols), _BF16)

    plans = [cast_plan(w, l) for w, l in cast]

    def kernel_body(*refs):
        ins, cast_in = refs[:n_in], refs[n_in:n_in + n_cast]
        o_ref, cast_out = refs[n_in + n_cast], refs[n_in + n_cast + 1:n_in + 2 * n_cast + 1]
        for src, dst in zip(cast_in, cast_out):
            dst[...] = src[...].astype(_BF16)
        body(*ins, o_ref, *refs[n_in + 2 * n_cast + 1:])

    outs = pl.pallas_call(
        kernel_body,
        grid=grid,
        in_specs=list(in_specs) + [p[1] for p in plans],
        out_specs=[out_spec] + [p[2] for p in plans],
        out_shape=[out_shape] + [p[3] for p in plans],
        scratch_shapes=scratch_shapes,
        compiler_params=pltpu.CompilerParams(
            dimension_semantics=("arbitrary", "arbitrary"),
            vmem_limit_bytes=VMEM_LIMIT_BYTES),
        name=name,
    )(*operands, *[p[0] for p in plans])
    return outs[0], [o.reshape(w.shape[1:]) for o, (w, _) in zip(outs[1:], cast)]


def _seq_tiled_call(body, x, ada, layer, weights, scratch_shapes, name, cast=()):
    batch, seq, d = x.shape
    assert seq % SEQ_TILE == 0 and SEQ_TILE % GMLP_BLOCK == 0 and SEQ_TILE % ROW_BLOCK == 0
    x_spec = pl.BlockSpec((1, SEQ_TILE, d), lambda b, s: (b, s, 0))
    ada_spec = pl.BlockSpec((None,) + ada.shape[1:], lambda b, s: (layer, 0, 0),
                            pipeline_mode=pl.Buffered(1))
    row_stacks = [2 + i for i, (w, l) in enumerate(weights) if _is_row_stack(w, l)]

    def layer_body(*refs):
        refs = list(refs)
        for i in row_stacks:
            per_layer = refs[i].shape[0] // DEPTH
            refs[i] = refs[i].at[layer * per_layer:(layer + 1) * per_layer]
        body(*refs)

    return _call_with_casts(
        layer_body, (batch, seq // SEQ_TILE),
        [x_spec, ada_spec] + [_resident_spec(w, l) for w, l in weights],
        x_spec, jax.ShapeDtypeStruct(x.shape, x.dtype),
        [x, ada] + [w for w, _ in weights], cast, name, scratch_shapes)


def _token_mix(x, ada, layer, weights, cast):
    ts = SEQ_TILE
    scratch = [pltpu.VMEM((ts, D_MODEL), _F32) for _ in range(4)] + [
        pltpu.VMEM((CONV_HALO + ts, D_CONV), _F32),
        pltpu.VMEM((POOL_HALO + ts, D_POOL), _F32),
        pltpu.VMEM((POOL_HALO + ts, POOL_GROUP), _F32),
        pltpu.VMEM((POOL_HALO + ts, POOL_GROUP), _F32),
        pltpu.VMEM((ts, D_CONV), _BF16),
        pltpu.VMEM((ts, D_GMLP), _BF16),
        pltpu.VMEM((ts, D_MODEL), _F32),
    ]
    return _seq_tiled_call(_token_mix_kernel, x, ada, layer, weights, scratch, "token_mix", cast)


def _channel_mix(x, ada, layer, weights, cast):
    ts = SEQ_TILE
    scratch = [
        pltpu.VMEM((ts, D_FF), _F32),
        pltpu.VMEM((CONV_HALO + ts, D_FF), _F32),
        pltpu.VMEM((ts, D_FF), _BF16),
        pltpu.VMEM((ts, D_MODEL), _F32),
    ]
    return _seq_tiled_call(_channel_mix_kernel, x, ada, layer, weights, scratch, "channel_mix",
                           cast)


def _ada_modulation(c, w_ada, b_ada, cast):
    depth, d, n_out = w_ada.shape
    batch = c.shape[0]
    rows = -(-batch // SUBLANES) * SUBLANES
    c_pad = jnp.pad(c, ((0, rows - batch), (0, 0)))
    col_block = ADA_COL_BLOCK
    assert n_out % col_block == 0
    return _call_with_casts(
        _ada_kernel, (depth, n_out // col_block),
        [pl.BlockSpec((rows, d), lambda l, j: (0, 0)),
         pl.BlockSpec((1, d, col_block), lambda l, j: (l, 0, j)),
         pl.BlockSpec((depth, col_block), lambda l, j: (0, j))],
        pl.BlockSpec((1, rows, col_block), lambda l, j: (l, 0, j)),
        jax.ShapeDtypeStruct((depth, rows, n_out), _F32),
        [c_pad, w_ada, b_ada], cast, "ada_modulation")


def kernel(x, c, w_ada, b_ada, w_in, b_in, conv_a, w_a_out, ln_v_g, ln_v_b, w_spatial, b_spatial, w_b_out, w_pool, pool_scale, w_o, ln1_g, ln1_b, w_up, b_up, conv_ffn, conv_ffn_b, w_down, ln2_g, ln2_b):
    depth = w_in.shape[0]
    token_f32 = (w_in, w_a_out, w_spatial, w_b_out, w_pool, w_o)
    channel_f32 = (w_up, w_down)
    ada, token_h = _ada_modulation(c, w_ada, b_ada, [(w, 0) for w in token_f32])

    b_sp = jnp.repeat(jnp.swapaxes(b_spatial, 1, 2), GMLP_GROUP, axis=2)

    def token_weights(layer, w_in_h, w_a_out_h, w_sp_h, w_b_out_h, w_pool_h, w_o_h):
        return [(w_in_h, None), (b_in, layer), (conv_a, layer), (w_a_out_h, None),
                (ln_v_g, layer), (ln_v_b, layer), (w_sp_h, None), (b_sp, layer),
                (w_b_out_h, None), (w_pool_h, None), (pool_scale, layer), (w_o_h, None),
                (ln1_g, layer), (ln1_b, layer)]

    def channel_weights(layer, w_up_h, w_down_h):
        return [(w_up_h, None), (b_up, layer), (conv_ffn, layer), (conv_ffn_b, layer),
                (w_down_h, None), (ln2_g, layer), (ln2_b, layer)]

    for layer in range(depth):
        x, channel_h = _token_mix(x, ada, layer, token_weights(layer, *token_h),
                                  [(w, layer) for w in channel_f32])
        nxt = [(w, layer + 1) for w in token_f32] if layer + 1 < depth else []
        x, token_h = _channel_mix(x, ada, layer, channel_weights(layer, *channel_h), nxt)
    return x
```
